```python
import jax, jax.numpy as jnp
from jax import lax
import numpy as np

D_MODEL = 2048
BATCH = 2
SEQ = 4096
DEPTH = 4

GRID_W = 64
CTX_LEN = 256

N_HEADS = 4
D_KEY = D_MODEL // 2
D_VAL = D_MODEL
HEAD_K = D_KEY // N_HEADS
HEAD_V = D_VAL // N_HEADS
GATE_RANK = 16
GATE_NORM = 16.0
CHUNK = 64

POOL_WINDOWS = (2, 4, 8, 16)
D_POOL = D_MODEL // 2
POOL_GROUP = D_POOL // len(POOL_WINDOWS)

D_FF = ((8 * D_MODEL + 3 * 256 - 1) // (3 * 256)) * 256

N_BRANCH = 2
N_MOD = 6
DEEPNORM_ALPHA = (2.0 * DEPTH) ** 0.25
DEEPNORM_BETA = (8.0 * DEPTH) ** -0.25
LN_EPS = 1e-5
RMS_EPS = 1e-6

PROJ_WIDTHS = (D_KEY, D_KEY, D_VAL, D_VAL, 2 * GATE_RANK, D_POOL, N_BRANCH * D_MODEL)
D_PROJ = sum(PROJ_WIDTHS)
SPLIT_POINTS = tuple(int(s) for s in np.cumsum(PROJ_WIDTHS)[:-1])

kernel_name = "hybrid_gla_pool_diffusion_trunk"


def _layer_norm(x, gain, bias):
    xf = x.astype(jnp.float32)
    mu = jnp.mean(xf, axis=-1, keepdims=True)
    var = jnp.mean(jnp.square(xf - mu), axis=-1, keepdims=True)
    y = (xf - mu) * lax.rsqrt(var + LN_EPS)
    return (y * gain + bias).astype(x.dtype)


def _modulation(cond, w_ada, b_ada):
    return jnp.split(jax.nn.silu(cond) @ w_ada + b_ada, N_MOD, axis=-1)


def _heads(t, head_dim):
    b, l, _ = t.shape
    return t.reshape(b, l, -1, head_dim).transpose(0, 2, 1, 3)


def _split_proj(proj, w_decay_up, b_decay_up):
    q, k, v, g, a_lr, p, bg = jnp.split(proj, SPLIT_POINTS, axis=-1)
    q = _heads(q * HEAD_K ** -0.5, HEAD_K)
    k = _heads(k, HEAD_K)
    v = _heads(v, HEAD_V)
    log_decay = []
    for d in range(2):
        z = a_lr[..., d * GATE_RANK:(d + 1) * GATE_RANK] @ w_decay_up[d] + b_decay_up[d]
        log_decay.append(_heads(jax.nn.log_sigmoid(z.astype(jnp.float32)) / GATE_NORM, HEAD_K))
    return q, k, v, log_decay[0], log_decay[1], g, p, bg


def _gla_chunked(q, k, v, log_a, s0):
    b_, h_, l_, _ = q.shape
    n = l_ // CHUNK

    def chunks(t):
        return t.reshape(b_, h_, n, CHUNK, t.shape[-1]).astype(jnp.float32)

    q, k, v, log_a = chunks(q), chunks(k), chunks(v), chunks(log_a)
    cum = jnp.cumsum(log_a, axis=3)
    ref = cum[:, :, :, CHUNK // 2 - 1:CHUNK // 2]
    q_in = q * jnp.exp(cum - ref)
    k_in = k * jnp.exp(ref - cum)
    scores = jnp.einsum('bhnid,bhnjd->bhnij', q_in, k_in)
    mask = jnp.tril(jnp.ones((CHUNK, CHUNK), dtype=bool))
    scores = jnp.where(mask, scores, 0.0)
    o_intra = jnp.einsum('bhnij,bhnjv->bhniv', scores, v)
    last = cum[:, :, :, -1:]
    q_inter = q * jnp.exp(cum)
    k_state = k * jnp.exp(last - cum)
    decay_chunk = jnp.exp(last[:, :, :, 0, :])
    xs = (jnp.moveaxis(q_inter, 2, 0), jnp.moveaxis(k_state, 2, 0),
          jnp.moveaxis(v, 2, 0), jnp.moveaxis(decay_chunk, 2, 0))

    def step(state, inp):
        qc, kc, vc, dc = inp
        o = jnp.einsum('bhid,bhdv->bhiv', qc, state)
        state = dc[..., None] * state + jnp.einsum('bhjd,bhjv->bhdv', kc, vc)
        return state, o

    s_final, o_inter = lax.scan(step, s0.astype(jnp.float32), xs)
    o = o_intra + jnp.moveaxis(o_inter, 0, 2)
    return o.reshape(b_, h_, l_, HEAD_V), s_final


def _gla_two_way(q, k, v, la_fwd, la_bwd, s0_fwd, s0_bwd):
    flip = lambda t: jnp.flip(t, axis=2)
    o_f, s_f = _gla_chunked(q, k, v, la_fwd, s0_fwd)
    o_b, s_b = _gla_chunked(flip(q), flip(k), flip(v), flip(la_bwd), s0_bwd)
    return o_f + flip(o_b), s_f, s_b


def _box_mean(t, axis, window):
    n = t.shape[axis]
    lo = window // 2
    hi = window - lo - 1
    cs = jnp.cumsum(t.astype(jnp.float32), axis=axis)
    zero = jnp.zeros_like(lax.slice_in_dim(cs, 0, 1, axis=axis))
    cs = jnp.concatenate([zero, cs], axis=axis)
    idx = jnp.arange(n)
    upper = jnp.minimum(idx + hi + 1, n)
    lower = jnp.maximum(idx - lo, 0)
    total = jnp.take(cs, upper, axis=axis) - jnp.take(cs, lower, axis=axis)
    shape = [1] * t.ndim
    shape[axis] = n
    count = (upper - lower).astype(jnp.float32).reshape(shape)
    return (total / count).astype(t.dtype)


def _pool_branch(p, rows, w_pool_group, pool_scale, w_pool_out):
    outs = []
    for i, w in enumerate(POOL_WINDOWS):
        pg = p[..., i * POOL_GROUP:(i + 1) * POOL_GROUP]
        if rows is None:
            mean = _box_mean(pg, 1, w)
        else:
            b, l, ch = pg.shape
            img = pg.reshape(b, rows, GRID_W, ch)
            mean = _box_mean(_box_mean(img, 2, w), 1, w).reshape(b, l, ch)
        outs.append((mean - pg) @ w_pool_group[i])
    return (jnp.concatenate(outs, axis=-1) * pool_scale) @ w_pool_out


def _gla_output(o, g, gain, w_gla_out):
    of = o.transpose(0, 2, 1, 3)
    of = of * lax.rsqrt(jnp.mean(of * of, axis=-1, keepdims=True) + RMS_EPS)
    b, l = of.shape[:2]
    of = of.reshape(b, l, D_VAL).astype(g.dtype) * gain
    return (of * jax.nn.silu(g)) @ w_gla_out


def _merge(o, g, p, bg, rows, gla_norm_gain, w_gla_out, w_pool_group, pool_scale, w_pool_out, w_out):
    y_gla = _gla_output(o, g, gla_norm_gain, w_gla_out)
    y_pool = _pool_branch(p, rows, w_pool_group, pool_scale, w_pool_out)
    gate_gla, gate_pool = jnp.split(jax.nn.sigmoid(bg), N_BRANCH, axis=-1)
    return (gate_gla * y_gla + gate_pool * y_pool) @ w_out


def _token_mixer(h_lat, h_ctx, rows, ctx_out, w_in, w_decay_up, b_decay_up, gla_norm_gain,
                 w_pool_group, pool_scale, w_gla_out, w_pool_out, w_out):
    q_c, k_c, v_c, laf_c, lab_c, g_c, p_c, bg_c = _split_proj(h_ctx @ w_in, w_decay_up, b_decay_up)
    q_l, k_l, v_l, laf_l, lab_l, g_l, p_l, bg_l = _split_proj(h_lat @ w_in, w_decay_up, b_decay_up)
    s0 = jnp.zeros((h_lat.shape[0], N_HEADS, HEAD_K, HEAD_V), jnp.float32)
    o_c, s_fwd, s_bwd = _gla_two_way(q_c, k_c, v_c, laf_c, lab_c, s0, s0)
    o_l, _, _ = _gla_two_way(q_l, k_l, v_l, laf_l, lab_l, s_fwd, s_bwd)
    y_lat = _merge(o_l, g_l, p_l, bg_l, rows, gla_norm_gain, w_gla_out, w_pool_group,
                   pool_scale, w_pool_out, w_out)
    y_ctx = None
    if ctx_out:
        y_ctx = _merge(o_c, g_c, p_c, bg_c, None, gla_norm_gain, w_gla_out, w_pool_group,
                       pool_scale, w_pool_out, w_out)
    return y_lat, y_ctx


def _swiglu(h, w_ffn_in, w_ffn_out):
    gate, up = jnp.split(h @ w_ffn_in, 2, axis=-1)
    return (jax.nn.silu(gate) * up) @ w_ffn_out


def setup_inputs(seed: int = 0) -> dict:
    key = jax.random.key(seed)
    ks = jax.random.split(key, 24)
    f32 = jnp.float32
    nrm = lambda k, shape, s: jax.random.normal(k, shape, f32) * s
    L = DEPTH
    return {
        "x": nrm(ks[0], (BATCH, SEQ, D_MODEL), 1.0),
        "c": nrm(ks[1], (BATCH, D_MODEL), 1.0),
        "ctx": nrm(ks[2], (BATCH, CTX_LEN, D_MODEL), 1.0),
        "c_ctx": nrm(ks[3], (D_MODEL,), 1.0),
        "w_ada": nrm(ks[4], (L, D_MODEL, N_MOD * D_MODEL), 0.5 * D_MODEL ** -0.5),
        "b_ada": nrm(ks[5], (L, N_MOD * D_MODEL), 0.02),
        "w_in": nrm(ks[6], (L, D_MODEL, D_PROJ), D_MODEL ** -0.5),
        "w_decay_up": nrm(ks[7], (L, 2, GATE_RANK, D_KEY), GATE_RANK ** -0.5),
        "b_decay_up": nrm(ks[8], (L, 2, D_KEY), 0.1),
        "gla_norm_gain": 1.0 + nrm(ks[9], (L, D_VAL), 0.1),
        "w_pool_group": nrm(ks[10], (L, len(POOL_WINDOWS), POOL_GROUP, POOL_GROUP), POOL_GROUP ** -0.5),
        "pool_scale": 1.0 + nrm(ks[11], (L, D_POOL), 0.1),
        "w_gla_out": nrm(ks[12], (L, D_VAL, D_MODEL), D_VAL ** -0.5),
        "w_pool_out": nrm(ks[13], (L, D_POOL, D_MODEL), D_POOL ** -0.5),
        "w_out": nrm(ks[14], (L, D_MODEL, D_MODEL), DEEPNORM_BETA * D_MODEL ** -0.5),
        "ln_mix_gain": 1.0 + nrm(ks[15], (L, D_MODEL), 0.1),
        "ln_mix_bias": nrm(ks[16], (L, D_MODEL), 0.02),
        "w_ffn_in": nrm(ks[17], (L, D_MODEL, 2 * D_FF), D_MODEL ** -0.5),
        "w_ffn_out": nrm(ks[18], (L, D_FF, D_MODEL), DEEPNORM_BETA * D_FF ** -0.5),
        "ln_ffn_gain": 1.0 + nrm(ks[19], (L, D_MODEL), 0.1),
        "ln_ffn_bias": nrm(ks[20], (L, D_MODEL), 0.02),
    }


def reference(x, c, ctx, c_ctx, w_ada, b_ada, w_in, w_decay_up, b_decay_up, gla_norm_gain,
              w_pool_group, pool_scale, w_gla_out, w_pool_out, w_out, ln_mix_gain, ln_mix_bias,
              w_ffn_in, w_ffn_out, ln_ffn_gain, ln_ffn_bias):
    rows = x.shape[1] // GRID_W
    for layer in range(DEPTH):
        ctx_out = layer < DEPTH - 1
        sh_m, sc_m, gt_m, sh_f, sc_f, gt_f = _modulation(c[:, None, :], w_ada[layer], b_ada[layer])
        csh_m, csc_m, cgt_m, csh_f, csc_f, cgt_f = _modulation(c_ctx, w_ada[layer], b_ada[layer])
        h_lat = x * (1.0 + sc_m) + sh_m
        h_ctx = ctx * (1.0 + csc_m) + csh_m
        mix_lat, mix_ctx = _token_mixer(h_lat, h_ctx, rows, ctx_out, w_in[layer], w_decay_up[layer],
                                        b_decay_up[layer], gla_norm_gain[layer], w_pool_group[layer],
                                        pool_scale[layer], w_gla_out[layer], w_pool_out[layer], w_out[layer])
        x = _layer_norm(DEEPNORM_ALPHA * x + gt_m * mix_lat, ln_mix_gain[layer], ln_mix_bias[layer])
        ffn_lat = _swiglu(x * (1.0 + sc_f) + sh_f, w_ffn_in[layer], w_ffn_out[layer])
        x = _layer_norm(DEEPNORM_ALPHA * x + gt_f * ffn_lat, ln_ffn_gain[layer], ln_ffn_bias[layer])
        if ctx_out:
            ctx = _layer_norm(DEEPNORM_ALPHA * ctx + cgt_m * mix_ctx, ln_mix_gain[layer], ln_mix_bias[layer])
            ffn_ctx = _swiglu(ctx * (1.0 + csc_f) + csh_f, w_ffn_in[layer], w_ffn_out[layer])
            ctx = _layer_norm(DEEPNORM_ALPHA * ctx + cgt_f * ffn_ctx, ln_ffn_gain[layer], ln_ffn_bias[layer])
    return x
```

```python
import functools

import numpy as np
import jax
import jax.numpy as jnp
from jax import lax
from jax.experimental import pallas as pl
from jax.experimental.pallas import tpu as pltpu

F32 = jnp.float32
BF16 = jnp.bfloat16

GRID_W = 64
N_HEADS = 4
GATE_RANK = 16
GATE_NORM = 16.0
CHUNK = 64
POOL_WINDOWS = (2, 4, 8, 16)
N_MOD = 6
LN_EPS = 1e-5
RMS_EPS = 1e-6

LANES = 128
VMEM_LIMIT = 56 * 1024 * 1024

GLA_BLOCK = 512
POOL_TILE = 256
LAT_TM = 1024
LN_TM = 512


def _cparams(n_axes, vmem=VMEM_LIMIT):
    return pltpu.CompilerParams(dimension_semantics=("arbitrary",) * n_axes,
                                vmem_limit_bytes=vmem)


def _silu(v):
    return v * jax.nn.sigmoid(v)


def _mods_kernel(cond_ref, w_ref, b_ref, o_ref):
    a = _silu(cond_ref[...]).astype(BF16)
    o_ref[...] = jnp.dot(a, w_ref[...].astype(BF16), preferred_element_type=F32) + b_ref[...]


def _mods_call(cond, w_ada, b_ada):
    depth, d, n = w_ada.shape
    tn = 1024
    return pl.pallas_call(
        _mods_kernel,
        out_shape=jax.ShapeDtypeStruct((depth, 8, n), F32),
        grid=(depth, n // tn),
        in_specs=[pl.BlockSpec((8, d), lambda l, j: (0, 0)),
                  pl.BlockSpec((None, d, tn), lambda l, j: (l, 0, j)),
                  pl.BlockSpec((None, 1, tn), lambda l, j: (l, 0, j))],
        out_specs=pl.BlockSpec((None, 8, tn), lambda l, j: (l, 0, j)),
        compiler_params=_cparams(2),
        name="mods",
    )(cond, w_ada, b_ada.reshape(depth, 1, n))


def _mod_spec(layer, chunk, d):
    return pl.BlockSpec((None, 8, d), lambda *_: (layer, 0, chunk))


def _modulate_kernel(x_ref, sc_ref, sh_ref, o_ref, *, row_fn):
    r = row_fn(pl.program_id(0))
    sc = sc_ref[pl.ds(r, 1), :]
    sh = sh_ref[pl.ds(r, 1), :]
    o_ref[...] = (x_ref[...] * (1.0 + sc) + sh).astype(BF16)


def _modulate_call(x, mods, layer, row_fn, tm):
    m, d = x.shape
    return pl.pallas_call(
        functools.partial(_modulate_kernel, row_fn=row_fn),
        out_shape=jax.ShapeDtypeStruct((m, d), BF16),
        grid=(m // tm,),
        in_specs=[pl.BlockSpec((tm, d), lambda i: (i, 0)),
                  _mod_spec(layer, 1, d), _mod_spec(layer, 0, d)],
        out_specs=pl.BlockSpec((tm, d), lambda i: (i, 0)),
        compiler_params=_cparams(1),
        name="modulate",
    )(x, mods, mods)


def _mm_kernel(a_ref, w_ref, o_ref):
    o_ref[...] = jnp.dot(a_ref[...], w_ref[...], preferred_element_type=F32).astype(o_ref.dtype)


def _mm_call(a, w, layer, tm, tn, out_dtype, name):
    m, k = a.shape
    n = w.shape[-1]
    return pl.pallas_call(
        _mm_kernel,
        out_shape=jax.ShapeDtypeStruct((m, n), out_dtype),
        grid=(n // tn, m // tm),
        in_specs=[pl.BlockSpec((tm, k), lambda j, i: (i, 0)),
                  pl.BlockSpec((None, k, tn), lambda j, i: (layer, 0, j))],
        out_specs=pl.BlockSpec((tm, tn), lambda j, i: (i, j)),
        compiler_params=_cparams(2),
        name=name,
    )(a, w)


def _gla_kernel(q_ref, k_ref, v_ref, a_ref, g_ref,
                qc_ref, kc_ref, vc_ref, ac_ref, gc_ref,
                wup_ref, bup_ref, gain_ref,
                out_ref, outc_ref,
                state_ref, acc_ref, accc_ref, *, nb, head_k):
    s = pl.program_id(2)
    n_blk = GLA_BLOCK // CHUNK
    n_ctx = qc_ref.shape[0] // CHUNK
    q_scale = head_k ** -0.5

    row = lax.broadcasted_iota(jnp.int32, (CHUNK, CHUNK), 0)
    col = lax.broadcasted_iota(jnp.int32, (CHUNK, CHUNK), 1)
    keep = (row >= col, row <= col)
    ref_row = (CHUNK // 2 - 1, CHUNK // 2)
    last_row = (CHUNK - 1, 0)

    def chunk_out(d, qr, kr, vr, ar, r0):
        tri = jnp.where(keep[d], 1.0, 0.0).astype(BF16)
        q = qr[pl.ds(r0, CHUNK), :].astype(F32) * q_scale
        k = kr[pl.ds(r0, CHUNK), :].astype(F32)
        v = vr[pl.ds(r0, CHUNK), :]
        a = ar[pl.ds(r0, CHUNK), :].astype(BF16)
        z = jnp.dot(a, wup_ref[d], preferred_element_type=F32) + bup_ref[d]
        log_a = (jnp.minimum(z, 0.0) - jnp.log(1.0 + jnp.exp(-jnp.abs(z)))) * (1.0 / GATE_NORM)
        hi = log_a.astype(BF16)
        rem = log_a - hi.astype(F32)
        mid = rem.astype(BF16)
        lo = (rem - mid.astype(F32)).astype(BF16)
        cum = (jnp.dot(tri, hi, preferred_element_type=F32)
               + jnp.dot(tri, mid, preferred_element_type=F32)
               + jnp.dot(tri, lo, preferred_element_type=F32))
        ref = cum[ref_row[d]:ref_row[d] + 1, :]
        last = cum[last_row[d]:last_row[d] + 1, :]
        q_in = (q * jnp.exp(cum - ref)).astype(BF16)
        k_in = (k * jnp.exp(ref - cum)).astype(BF16)
        scores = lax.dot_general(q_in, k_in, (((1,), (1,)), ((), ())), preferred_element_type=F32)
        scores = jnp.where(keep[d], scores, 0.0).astype(BF16)
        o = jnp.dot(scores, v, preferred_element_type=F32)
        q_inter = (q * jnp.exp(cum)).astype(BF16)
        k_state = (k * jnp.exp(last - cum)).astype(BF16)
        st = state_ref[...]
        o = o + lax.dot_general(q_inter, st.astype(BF16), (((1,), (1,)), ((), ())),
                                preferred_element_type=F32)
        upd = lax.dot_general(v, k_state, (((0,), (0,)), ((), ())), preferred_element_type=F32)
        state_ref[...] = st * jnp.exp(last) + upd
        return o

    def finalize(o, gr, outr, r0):
        ms = jnp.mean(o * o, axis=-1, keepdims=True)
        on = o * lax.rsqrt(ms + RMS_EPS)
        g = gr[pl.ds(r0, CHUNK), :].astype(F32)
        outr[pl.ds(r0, CHUNK), :] = (on * gain_ref[...] * _silu(g)).astype(outr.dtype)

    def scan_block(d, n_chunks, qr, kr, vr, ar, gr, outr, acc, acc_off):
        def body(c, carry):
            cc = (n_chunks - 1 - c) if d == 1 else c
            r0 = pl.multiple_of(cc * CHUNK, CHUNK)
            ra = pl.multiple_of(acc_off + cc * CHUNK, CHUNK)
            o = chunk_out(d, qr, kr, vr, ar, r0)
            if d == 1:
                acc[pl.ds(ra, CHUNK), :] = o
            else:
                finalize(acc[pl.ds(ra, CHUNK), :] + o, gr, outr, r0)
            return carry

        lax.fori_loop(0, n_chunks, body, 0)

    @pl.when(s < nb)
    def _():
        @pl.when(s == 0)
        def _():
            state_ref[...] = jnp.zeros_like(state_ref)
            scan_block(1, n_ctx, qc_ref, kc_ref, vc_ref, ac_ref, gc_ref, outc_ref, accc_ref, 0)

        scan_block(1, n_blk, q_ref, k_ref, v_ref, a_ref, g_ref, out_ref, acc_ref,
                   (nb - 1 - s) * GLA_BLOCK)

    @pl.when(s >= nb)
    def _():
        @pl.when(s == nb)
        def _():
            state_ref[...] = jnp.zeros_like(state_ref)
            scan_block(0, n_ctx, qc_ref, kc_ref, vc_ref, ac_ref, gc_ref, outc_ref, accc_ref, 0)

        scan_block(0, n_blk, q_ref, k_ref, v_ref, a_ref, g_ref, out_ref, acc_ref,
                   (s - nb) * GLA_BLOCK)


def _gla_call(proj_l, alr_l, proj_c, alr_c, wup, bup, gain, layer, batch, seq, ctx_len, d_model):
    d_key = d_model // 2
    head_k = d_key // N_HEADS
    head_v = d_model // N_HEADS
    nb = seq // GLA_BLOCK
    kq = d_key // head_k
    kv = (2 * d_key) // head_v
    kg = kv + d_model // head_v

    def blk(s):
        return jnp.where(s < nb, nb - 1 - s, s - nb)

    def lat(width, col0):
        return pl.BlockSpec((GLA_BLOCK, width), lambda b, h, s: (b * nb + blk(s), col0 + h))

    def lat_fwd_only(width, col0):
        return pl.BlockSpec((GLA_BLOCK, width),
                            lambda b, h, s: (b * nb + jnp.maximum(s - nb, 0), col0 + h))

    def cx(width, col0):
        return pl.BlockSpec((ctx_len, width), lambda b, h, s: (b, col0 + h))

    in_specs = [lat(head_k, 0), lat(head_k, kq), lat(head_v, kv),
                pl.BlockSpec((GLA_BLOCK, LANES), lambda b, h, s: (b * nb + blk(s), 0)),
                lat_fwd_only(head_v, kg),
                cx(head_k, 0), cx(head_k, kq), cx(head_v, kv),
                pl.BlockSpec((ctx_len, LANES), lambda b, h, s: (b, 0)),
                cx(head_v, kg),
                pl.BlockSpec((None, 2, LANES, head_k), lambda b, h, s: (layer, 0, 0, h)),
                pl.BlockSpec((None, 2, 1, head_k), lambda b, h, s: (layer, 0, 0, h)),
                pl.BlockSpec((None, 1, head_v), lambda b, h, s: (layer, 0, h))]
    out_specs = [lat_fwd_only(head_v, 0), cx(head_v, 0)]
    return pl.pallas_call(
        functools.partial(_gla_kernel, nb=nb, head_k=head_k),
        out_shape=(jax.ShapeDtypeStruct((batch * seq, d_model), BF16),
                   jax.ShapeDtypeStruct((batch * ctx_len, d_model), BF16)),
        grid=(batch, N_HEADS, 2 * nb),
        in_specs=in_specs,
        out_specs=out_specs,
        scratch_shapes=[pltpu.VMEM((head_v, head_k), F32),
                        pltpu.VMEM((seq, head_v), F32),
                        pltpu.VMEM((ctx_len, head_v), F32)],
        compiler_params=_cparams(3),
        name="gla",
    )(proj_l, proj_l, proj_l, alr_l, proj_l, proj_c, proj_c, proj_c, alr_c, proj_c, wup, bup, gain)


def _win(w):
    lo = w // 2
    return lo, w - lo - 1


def _band_matrix(n_tokens, period, w):
    lo, hi = _win(w)
    t = np.arange(n_tokens)
    same = (t[:, None] // period) == (t[None, :] // period)
    off = t[None, :] - t[:, None]
    return (same & (off >= -lo) & (off <= hi)).astype(np.float32)


def _count(idx, n, w):
    lo, hi = _win(w)
    return jnp.minimum(idx + hi + 1, n) - jnp.maximum(idx - lo, 0)


def _pool_lat_kernel(p_ref, band_ref, wg_ref, sc_ref, o_ref, ws_ref, *, n_rows):
    g = pl.program_id(1)
    seq = n_rows * GRID_W
    halo = 8 * GRID_W
    n_tiles = seq // POOL_TILE
    zeros = jnp.zeros((halo, ws_ref.shape[1]), F32)
    ws_ref[pl.ds(0, halo), :] = zeros
    ws_ref[pl.ds(halo + seq, halo), :] = zeros

    def wpass(t, carry):
        r0 = pl.multiple_of(t * POOL_TILE, POOL_TILE)
        ws_ref[pl.ds(halo + r0, POOL_TILE), :] = jnp.dot(
            band_ref[...], p_ref[pl.ds(r0, POOL_TILE), :], preferred_element_type=F32)
        return carry

    lax.fori_loop(0, n_tiles, wpass, 0)

    for gi, w in enumerate(POOL_WINDOWS):
        lo, hi = _win(w)

        @pl.when(g == gi)
        def _(lo=lo, hi=hi, w=w):
            def hpass(t, carry):
                r0 = pl.multiple_of(t * POOL_TILE, POOL_TILE)
                tot = ws_ref[pl.ds(halo + r0 - lo * GRID_W, POOL_TILE), :]
                for kk in range(-lo + 1, hi + 1):
                    tot = tot + ws_ref[pl.ds(halo + r0 + kk * GRID_W, POOL_TILE), :]
                tok = r0 + lax.broadcasted_iota(jnp.int32, (POOL_TILE, 1), 0)
                img_row = lax.shift_right_logical(tok, GRID_W.bit_length() - 1)
                img_col = jnp.bitwise_and(tok, GRID_W - 1)
                cnt = _count(img_row, n_rows, w) * _count(img_col, GRID_W, w)
                pin = p_ref[pl.ds(r0, POOL_TILE), :].astype(F32)
                diff = (tot / cnt.astype(F32) - pin).astype(BF16)
                y = jnp.dot(diff, wg_ref[...], preferred_element_type=F32) * sc_ref[...]
                o_ref[pl.ds(r0, POOL_TILE), :] = y.astype(o_ref.dtype)
                return carry

            lax.fori_loop(0, n_tiles, hpass, 0)


def _pool_lat_call(proj_l, band, wpg, pscale, layer, batch, seq, d_model):
    d_pool = d_model // 2
    grp = d_pool // len(POOL_WINDOWS)
    col0 = (d_model // 2 * 2 + 2 * d_model) // grp
    halo = 8 * GRID_W
    return pl.pallas_call(
        functools.partial(_pool_lat_kernel, n_rows=seq // GRID_W),
        out_shape=jax.ShapeDtypeStruct((batch * seq, d_pool), BF16),
        grid=(batch, len(POOL_WINDOWS)),
        in_specs=[pl.BlockSpec((seq, grp), lambda b, g: (b, col0 + g)),
                  pl.BlockSpec((None, POOL_TILE, POOL_TILE), lambda b, g: (g, 0, 0)),
                  pl.BlockSpec((None, None, grp, grp), lambda b, g: (layer, g, 0, 0)),
                  pl.BlockSpec((None, 1, grp), lambda b, g: (layer, 0, g))],
        out_specs=pl.BlockSpec((seq, grp), lambda b, g: (b, g)),
        scratch_shapes=[pltpu.VMEM((seq + 2 * halo, grp), F32)],
        compiler_params=_cparams(2),
        name="pool_lat",
    )(proj_l, band, wpg, pscale)


def _pool_ctx_kernel(p_ref, band_ref, icnt_ref, wg_ref, sc_ref, o_ref):
    tot = jnp.dot(band_ref[...], p_ref[...], preferred_element_type=F32)
    diff = (tot * icnt_ref[...] - p_ref[...].astype(F32)).astype(BF16)
    y = jnp.dot(diff, wg_ref[...], preferred_element_type=F32) * sc_ref[...]
    o_ref[...] = y.astype(o_ref.dtype)


def _pool_ctx_call(proj_c, band, icnt, wpg, pscale, layer, batch, ctx_len, d_model):
    d_pool = d_model // 2
    grp = d_pool // len(POOL_WINDOWS)
    col0 = (d_model // 2 * 2 + 2 * d_model) // grp
    return pl.pallas_call(
        _pool_ctx_kernel,
        out_shape=jax.ShapeDtypeStruct((batch * ctx_len, d_pool), BF16),
        grid=(batch, len(POOL_WINDOWS)),
        in_specs=[pl.BlockSpec((ctx_len, grp), lambda b, g: (b, col0 + g)),
                  pl.BlockSpec((None, ctx_len, ctx_len), lambda b, g: (g, 0, 0)),
                  pl.BlockSpec((None, ctx_len, 1), lambda b, g: (g, 0, 0)),
                  pl.BlockSpec((None, None, grp, grp), lambda b, g: (layer, g, 0, 0)),
                  pl.BlockSpec((None, 1, grp), lambda b, g: (layer, 0, g))],
        out_specs=pl.BlockSpec((ctx_len, grp), lambda b, g: (b, g)),
        compiler_params=_cparams(2),
        name="pool_ctx",
    )(proj_c, band, icnt, wpg, pscale)


def _merge_kernel(ga_ref, pa_ref, bgg_ref, bgp_ref, wg_ref, wp_ref, o_ref):
    yg = jnp.dot(ga_ref[...], wg_ref[...], preferred_element_type=F32)
    yp = jnp.dot(pa_ref[...], wp_ref[...], preferred_element_type=F32)
    y = (jax.nn.sigmoid(bgg_ref[...].astype(F32)) * yg
         + jax.nn.sigmoid(bgp_ref[...].astype(F32)) * yp)
    o_ref[...] = y.astype(o_ref.dtype)


def _merge_call(gla_act, pool_act, proj, w_gla_out, w_pool_out, layer, tm, tn, n_rows=None):
    m = gla_act.shape[0] if n_rows is None else n_rows
    d = w_gla_out.shape[-1]
    d_pool = pool_act.shape[1]
    bg0 = (proj.shape[1] - 2 * d) // tn
    return pl.pallas_call(
        _merge_kernel,
        out_shape=jax.ShapeDtypeStruct((m, d), BF16),
        grid=(d // tn, m // tm),
        in_specs=[pl.BlockSpec((tm, d), lambda j, i: (i, 0)),
                  pl.BlockSpec((tm, d_pool), lambda j, i: (i, 0)),
                  pl.BlockSpec((tm, tn), lambda j, i: (i, bg0 + j)),
                  pl.BlockSpec((tm, tn), lambda j, i: (i, bg0 + d // tn + j)),
                  pl.BlockSpec((None, d, tn), lambda j, i: (layer, 0, j)),
                  pl.BlockSpec((None, d_pool, tn), lambda j, i: (layer, 0, j))],
        out_specs=pl.BlockSpec((tm, tn), lambda j, i: (i, j)),
        compiler_params=_cparams(2),
        name="merge",
    )(gla_act, pool_act, proj, proj, w_gla_out, w_pool_out)


def _mm_ln_kernel(*refs, row_fn, nk, alpha, emit_h):
    if emit_h:
        (a_ref, w_ref, x_ref, gt_ref, lng_ref, lnb_ref, sc_ref, sh_ref, xo_ref, ho_ref, acc_ref) = refs
    else:
        (a_ref, w_ref, x_ref, gt_ref, lng_ref, lnb_ref, xo_ref, acc_ref) = refs
    kk = pl.program_id(1)
    part = jnp.dot(a_ref[...], w_ref[...], preferred_element_type=F32)

    @pl.when(kk == 0)
    def _():
        acc_ref[...] = part

    @pl.when(kk > 0)
    def _():
        acc_ref[...] += part

    @pl.when(kk == nk - 1)
    def _():
        r = row_fn(pl.program_id(0))
        y = alpha * x_ref[...] + gt_ref[pl.ds(r, 1), :] * acc_ref[...]
        mu = jnp.mean(y, axis=-1, keepdims=True)
        yc = y - mu
        var = jnp.mean(yc * yc, axis=-1, keepdims=True)
        xn = yc * lax.rsqrt(var + LN_EPS) * lng_ref[...] + lnb_ref[...]
        xo_ref[...] = xn
        if emit_h:
            ho_ref[...] = (xn * (1.0 + sc_ref[pl.ds(r, 1), :]) + sh_ref[pl.ds(r, 1), :]).astype(BF16)


def _mm_ln_call(a, w, x, mods, ln_gain, ln_bias, layer, gate_chunk, next_mod, row_fn, tm, tk, alpha,
                n_rows=None):
    m = x.shape[0] if n_rows is None else n_rows
    k = a.shape[1]
    d = x.shape[1]
    nk = k // tk
    emit_h = next_mod is not None
    in_specs = [pl.BlockSpec((tm, tk), lambda i, kk: (i, kk)),
                pl.BlockSpec((None, tk, d), lambda i, kk: (layer, kk, 0)),
                pl.BlockSpec((tm, d), lambda i, kk: (i, 0)),
                _mod_spec(layer, gate_chunk, d),
                pl.BlockSpec((None, 1, d), lambda i, kk: (layer, 0, 0)),
                pl.BlockSpec((None, 1, d), lambda i, kk: (layer, 0, 0))]
    args = [a, w, x, mods, ln_gain, ln_bias]
    out_shape = [jax.ShapeDtypeStruct((m, d), F32)]
    out_specs = [pl.BlockSpec((tm, d), lambda i, kk: (i, 0))]
    if emit_h:
        nl, sc_chunk, sh_chunk = next_mod
        in_specs += [_mod_spec(nl, sc_chunk, d), _mod_spec(nl, sh_chunk, d)]
        args += [mods, mods]
        out_shape.append(jax.ShapeDtypeStruct((m, d), BF16))
        out_specs.append(pl.BlockSpec((tm, d), lambda i, kk: (i, 0)))
    outs = pl.pallas_call(
        functools.partial(_mm_ln_kernel, row_fn=row_fn, nk=nk, alpha=alpha, emit_h=emit_h),
        out_shape=out_shape,
        grid=(m // tm, nk),
        in_specs=in_specs,
        out_specs=out_specs,
        scratch_shapes=[pltpu.VMEM((tm, d), F32)],
        compiler_params=_cparams(2),
        name="mm_ln",
    )(*args)
    return (outs[0], outs[1]) if emit_h else (outs[0], None)


def _ffn1_kernel(h_ref, wg_ref, wu_ref, o_ref):
    h = h_ref[...]
    gate = jnp.dot(h, wg_ref[...], preferred_element_type=F32)
    up = jnp.dot(h, wu_ref[...], preferred_element_type=F32)
    o_ref[...] = (_silu(gate) * up).astype(o_ref.dtype)


def _ffn1_call(h, w_ffn_in, layer, tm, tn, n_rows=None):
    m = h.shape[0] if n_rows is None else n_rows
    d = h.shape[1]
    d_ff = w_ffn_in.shape[-1] // 2
    nj = d_ff // tn
    return pl.pallas_call(
        _ffn1_kernel,
        out_shape=jax.ShapeDtypeStruct((m, d_ff), BF16),
        grid=(nj, m // tm),
        in_specs=[pl.BlockSpec((tm, d), lambda j, i: (i, 0)),
                  pl.BlockSpec((None, d, tn), lambda j, i: (layer, 0, j)),
                  pl.BlockSpec((None, d, tn), lambda j, i: (layer, 0, nj + j))],
        out_specs=pl.BlockSpec((tm, tn), lambda j, i: (i, j)),
        compiler_params=_cparams(2),
        name="ffn1",
    )(h, w_ffn_in, w_ffn_in)


def kernel(x, c, ctx, c_ctx, w_ada, b_ada, w_in, w_decay_up, b_decay_up, gla_norm_gain, w_pool_group, pool_scale, w_gla_out, w_pool_out, w_out, ln_mix_gain, ln_mix_bias, w_ffn_in, w_ffn_out, ln_ffn_gain, ln_ffn_bias):
    batch, seq, d = x.shape
    ctx_len = ctx.shape[1]
    depth = w_ada.shape[0]
    d_key = d // 2
    d_pool = d // 2
    d_ff = w_ffn_out.shape[1]
    alpha = (2.0 * depth) ** 0.25
    m_lat = batch * seq
    m_ctx = batch * ctx_len
    assert seq % (2 * GLA_BLOCK) == 0 and seq % LAT_TM == 0 and ctx_len % CHUNK == 0
    assert batch + 1 <= 8

    x_l = x.reshape(m_lat, d)
    x_c = ctx.reshape(m_ctx, d)
    cond = jnp.zeros((8, d), F32).at[:batch].set(c).at[batch].set(c_ctx)
    a0 = 2 * d_key + 2 * d
    a1 = a0 + 2 * GATE_RANK
    w_main = jnp.concatenate([w_in[:, :, :a0], w_in[:, :, a1:]], axis=-1).astype(BF16)
    w_alr = jnp.pad(w_in[:, :, a0:a1], ((0, 0), (0, 0), (0, LANES - 2 * GATE_RANK))).astype(BF16)
    wup = jnp.zeros((depth, 2, LANES, d_key), F32)
    wup = wup.at[:, 0, :GATE_RANK].set(w_decay_up[:, 0]).at[:, 1, GATE_RANK:2 * GATE_RANK].set(w_decay_up[:, 1])
    wup = wup.astype(BF16)
    bup = b_decay_up.reshape(depth, 2, 1, d_key)
    gain = gla_norm_gain.reshape(depth, 1, d)
    wpg = w_pool_group.astype(BF16)
    pscale = pool_scale.reshape(depth, 1, d_pool)
    w_gla_out_b = w_gla_out.astype(BF16)
    w_pool_out_b = w_pool_out.astype(BF16)
    w_out_b = w_out.astype(BF16)
    w_ffn_in_b = w_ffn_in.astype(BF16)
    w_ffn_out_b = w_ffn_out.astype(BF16)
    lnm_g = ln_mix_gain.reshape(depth, 1, d)
    lnm_b = ln_mix_bias.reshape(depth, 1, d)
    lnf_g = ln_ffn_gain.reshape(depth, 1, d)
    lnf_b = ln_ffn_bias.reshape(depth, 1, d)
    band_lat = jnp.asarray(np.stack([_band_matrix(POOL_TILE, GRID_W, w) for w in POOL_WINDOWS]), BF16)
    band_ctx = jnp.asarray(np.stack([_band_matrix(ctx_len, ctx_len, w) for w in POOL_WINDOWS]), BF16)
    t = np.arange(ctx_len)
    icnt_ctx = jnp.asarray(np.stack([
        1.0 / (np.minimum(t + _win(w)[1] + 1, ctx_len) - np.maximum(t - _win(w)[0], 0))
        for w in POOL_WINDOWS]).astype(np.float32)[:, :, None])

    mods = _mods_call(cond, w_ada, b_ada)

    lat_row_1024 = lambda i: i // (seq // LAT_TM)
    lat_row_512 = lambda i: i // (seq // LN_TM)
    ctx_row = lambda i: batch

    h_l = _modulate_call(x_l, mods, 0, lat_row_1024, LAT_TM)
    h_c = _modulate_call(x_c, mods, 0, ctx_row, m_ctx)

    ffn_tk = d_ff // 4
    for layer in range(depth):
        ctx_out = layer < depth - 1
        proj_l = _mm_call(h_l, w_main, layer, LAT_TM, 1024, BF16, "in_proj")
        alr_l = _mm_call(h_l, w_alr, layer, LAT_TM, LANES, F32, "in_proj_decay")
        proj_c = _mm_call(h_c, w_main, layer, m_ctx, 1024, BF16, "in_proj_ctx")
        alr_c = _mm_call(h_c, w_alr, layer, m_ctx, LANES, F32, "in_proj_decay_ctx")
        gla_l, gla_c = _gla_call(proj_l, alr_l, proj_c, alr_c, wup, bup, gain, layer,
                                 batch, seq, ctx_len, d)
        pool_l = _pool_lat_call(proj_l, band_lat, wpg, pscale, layer, batch, seq, d)
        merged_l = _merge_call(gla_l, pool_l, proj_l, w_gla_out_b, w_pool_out_b, layer, LAT_TM, 1024)
        x_l, hf_l = _mm_ln_call(merged_l, w_out_b, x_l, mods, lnm_g, lnm_b, layer, 2,
                                (layer, 4, 3), lat_row_512, LN_TM, d, alpha)
        act_l = _ffn1_call(hf_l, w_ffn_in_b, layer, LAT_TM, 512)
        nxt = (layer + 1, 1, 0) if ctx_out else None
        x_l, h_l = _mm_ln_call(act_l, w_ffn_out_b, x_l, mods, lnf_g, lnf_b, layer, 5,
                               nxt, lat_row_512, LN_TM, ffn_tk, alpha)
        if ctx_out:
            pool_c = _pool_ctx_call(proj_c, band_ctx, icnt_ctx, wpg, pscale, layer, batch, ctx_len, d)
            merged_c = _merge_call(gla_c, pool_c, proj_c, w_gla_out_b, w_pool_out_b, layer, m_ctx, 1024)
            x_c, hf_c = _mm_ln_call(merged_c, w_out_b, x_c, mods, lnm_g, lnm_b, layer, 2,
                                    (layer, 4, 3), ctx_row, m_ctx, d, alpha)
            act_c = _ffn1_call(hf_c, w_ffn_in_b, layer, m_ctx, 512)
            x_c, h_c = _mm_ln_call(act_c, w_ffn_out_b, x_c, mods, lnf_g, lnf_b, layer, 5,
                                   (layer + 1, 1, 0), ctx_row, m_ctx, ffn_tk, alpha)
    return x_l.reshape(batch, seq, d)
```

```python
import functools

import numpy as np
import jax
import jax.numpy as jnp
from jax import lax
from jax.experimental import pallas as pl
from jax.experimental.pallas import tpu as pltpu

F32 = jnp.float32
BF16 = jnp.bfloat16

GRID_W = 64
N_HEADS = 4
GATE_RANK = 16
GATE_NORM = 16.0
CHUNK = 64
POOL_WINDOWS = (2, 4, 8, 16)
N_MOD = 6
LN_EPS = 1e-5
RMS_EPS = 1e-6
LOG2E = 1.4426950408889634

LANES = 128
VMEM_LIMIT = 56 * 1024 * 1024

GLA_BLOCK = 512
GLA_FINAL_ROWS = 2048
POOL_TILE = 256
LAT_TM = 1024
LN_TM = 512


def _cparams(n_axes, vmem=VMEM_LIMIT):
    return pltpu.CompilerParams(dimension_semantics=("arbitrary",) * n_axes,
                                vmem_limit_bytes=vmem)


def _silu(v):
    return v * jax.nn.sigmoid(v)


def _mods_kernel(cond_ref, w_ref, b_ref, o_ref):
    a = _silu(cond_ref[...]).astype(BF16)
    o_ref[...] = jnp.dot(a, w_ref[...].astype(BF16), preferred_element_type=F32) + b_ref[...]


def _mods_call(cond, w_ada, b_ada):
    depth, d, n = w_ada.shape
    tn = 1024
    return pl.pallas_call(
        _mods_kernel,
        out_shape=jax.ShapeDtypeStruct((depth, 8, n), F32),
        grid=(depth, n // tn),
        in_specs=[pl.BlockSpec((8, d), lambda l, j: (0, 0)),
                  pl.BlockSpec((None, d, tn), lambda l, j: (l, 0, j)),
                  pl.BlockSpec((None, 1, tn), lambda l, j: (l, 0, j))],
        out_specs=pl.BlockSpec((None, 8, tn), lambda l, j: (l, 0, j)),
        compiler_params=_cparams(2),
        name="mods",
    )(cond, w_ada, b_ada.reshape(depth, 1, n))


def _mod_spec(layer, chunk, d):
    return pl.BlockSpec((None, 8, d), lambda *_: (layer, 0, chunk))


def _modulate_kernel(x_ref, sc_ref, sh_ref, o_ref, *, row_fn):
    r = row_fn(pl.program_id(0))
    sc = sc_ref[pl.ds(r, 1), :]
    sh = sh_ref[pl.ds(r, 1), :]
    o_ref[...] = (x_ref[...] * (1.0 + sc) + sh).astype(BF16)


def _modulate_call(x, mods, layer, row_fn, tm):
    m, d = x.shape
    return pl.pallas_call(
        functools.partial(_modulate_kernel, row_fn=row_fn),
        out_shape=jax.ShapeDtypeStruct((m, d), BF16),
        grid=(m // tm,),
        in_specs=[pl.BlockSpec((tm, d), lambda i: (i, 0)),
                  _mod_spec(layer, 1, d), _mod_spec(layer, 0, d)],
        out_specs=pl.BlockSpec((tm, d), lambda i: (i, 0)),
        compiler_params=_cparams(1),
        name="modulate",
    )(x, mods, mods)


def _mm_kernel(a_ref, w_ref, o_ref):
    o_ref[...] = jnp.dot(a_ref[...], w_ref[...], preferred_element_type=F32).astype(o_ref.dtype)


def _mm_call(a, w, layer, tm, tn, out_dtype, name):
    m, k = a.shape
    n = w.shape[-1]
    return pl.pallas_call(
        _mm_kernel,
        out_shape=jax.ShapeDtypeStruct((m, n), out_dtype),
        grid=(n // tn, m // tm),
        in_specs=[pl.BlockSpec((tm, k), lambda j, i: (i, 0)),
                  pl.BlockSpec((None, k, tn), lambda j, i: (layer, 0, j))],
        out_specs=pl.BlockSpec((tm, tn), lambda j, i: (i, j)),
        compiler_params=_cparams(2),
        name=name,
    )(a, w)


def _in_proj_kernel(a_ref, w_ref, gain_ref, o_ref, *, n_q, g_lo, g_hi, q_scale):
    j = pl.program_id(0)
    is_q = j < n_q
    is_g = jnp.logical_and(j >= g_lo, j < g_hi)

    def mm():
        return jnp.dot(a_ref[...], w_ref[...], preferred_element_type=F32)

    @pl.when(is_q)
    def _():
        o_ref[...] = (mm() * q_scale).astype(o_ref.dtype)

    @pl.when(is_g)
    def _():
        o_ref[...] = (_silu(mm()) * gain_ref[...]).astype(o_ref.dtype)

    @pl.when(jnp.logical_not(jnp.logical_or(is_q, is_g)))
    def _():
        o_ref[...] = mm().astype(o_ref.dtype)


def _in_proj_call(a, w, gain, layer, tm, tn, name):
    m, k = a.shape
    n = w.shape[-1]
    d = gain.shape[-1]
    d_key = d // 2
    g_lo = (2 * d_key + d) // tn
    g_hi = g_lo + d // tn
    kern = functools.partial(_in_proj_kernel, n_q=d_key // tn, g_lo=g_lo, g_hi=g_hi,
                             q_scale=(d_key // N_HEADS) ** -0.5)
    return pl.pallas_call(
        kern,
        out_shape=jax.ShapeDtypeStruct((m, n), BF16),
        grid=(n // tn, m // tm),
        in_specs=[pl.BlockSpec((tm, k), lambda j, i: (i, 0)),
                  pl.BlockSpec((None, k, tn), lambda j, i: (layer, 0, j)),
                  pl.BlockSpec((None, 1, tn),
                               lambda j, i: (layer, 0, jnp.clip(j - g_lo, 0, g_hi - g_lo - 1)))],
        out_specs=pl.BlockSpec((tm, tn), lambda j, i: (i, j)),
        compiler_params=_cparams(2),
        name=name,
    )(a, w, gain)


def _gla_kernel(qf_ref, kf_ref, vf_ref, af_ref, qb_ref, kb_ref, vb_ref, ab_ref, g_ref,
                qc_ref, kc_ref, vc_ref, ac_ref, gc_ref,
                wup_ref, bup_ref,
                out_ref, outc_ref,
                state_ref, acc_ref, accc_ref, *, nb):
    s = pl.program_id(2)
    fin_rows = out_ref.shape[0]

    row = lax.broadcasted_iota(jnp.int32, (CHUNK, CHUNK), 0)
    col = lax.broadcasted_iota(jnp.int32, (CHUNK, CHUNK), 1)
    keep = (row >= col, row <= col)
    ref_row = (CHUNK // 2 - 1, CHUNK // 2)
    last_row = (CHUNK - 1, 0)

    def log_decay(d, ar):
        z = jnp.dot(ar[...].astype(BF16), wup_ref[d], preferred_element_type=F32) + bup_ref[d]
        log_a = ((jnp.minimum(z, 0.0) - jnp.log(1.0 + jnp.exp2(jnp.abs(z) * (-LOG2E))))
                 * (LOG2E / GATE_NORM))
        hi = log_a.astype(BF16)
        return hi, (log_a - hi.astype(F32)).astype(BF16)

    def intra(d, tri, hi, lo, qr, kr, vr, c):
        sl = slice(c * CHUNK, (c + 1) * CHUNK)
        cum = (jnp.dot(tri, hi[sl], preferred_element_type=F32)
               + jnp.dot(tri, lo[sl], preferred_element_type=F32))
        ref = cum[ref_row[d]:ref_row[d] + 1, :]
        last = cum[last_row[d]:last_row[d] + 1, :]
        q = qr[sl, :].astype(F32)
        k = kr[sl, :].astype(F32)
        v = vr[sl, :]
        q_in = (q * jnp.exp2(cum - ref)).astype(BF16)
        k_in = (k * jnp.exp2(ref - cum)).astype(BF16)
        scores = lax.dot_general(q_in, k_in, (((1,), (1,)), ((), ())), preferred_element_type=F32)
        scores = jnp.where(keep[d], scores, 0.0).astype(BF16)
        o_intra = jnp.dot(scores, v, preferred_element_type=F32)
        q_inter = (q * jnp.exp2(cum)).astype(BF16)
        k_state = (k * jnp.exp2(last - cum)).astype(BF16)
        upd = lax.dot_general(v, k_state, (((0,), (0,)), ((), ())), preferred_element_type=F32)
        return o_intra, q_inter, upd, jnp.exp2(last)

    def scan_pair(fwd, bwd, acc, off_f, off_b):
        streams = []
        for d, (qr, kr, vr, ar), off in ((0, fwd, off_f), (1, bwd, off_b)):
            n_chunks = qr.shape[0] // CHUNK
            order = list(range(n_chunks - 1, -1, -1) if d == 1 else range(n_chunks))
            hi, lo = log_decay(d, ar)
            tri = jnp.where(keep[d], 1.0, 0.0).astype(BF16)
            streams.append((d, tri, qr, kr, vr, hi, lo, order, off))
        n_steps = len(streams[0][7])
        pre = {}
        for i in range(n_steps):
            for d, tri, qr, kr, vr, hi, lo, order, _ in streams:
                pre[d, i] = intra(d, tri, hi, lo, qr, kr, vr, order[i])
        st = [state_ref[0], state_ref[1]]
        for i in range(n_steps):
            for d, _, _, _, _, _, _, order, off in streams:
                o_intra, q_inter, upd, decay = pre[d, i]
                o = o_intra + lax.dot_general(q_inter, st[d].astype(BF16), (((1,), (1,)), ((), ())),
                                              preferred_element_type=F32)
                st[d] = st[d] * decay + upd
                ra = pl.multiple_of(off + order[i] * CHUNK, CHUNK)
                acc[pl.ds(ra, CHUNK), :] += o
        state_ref[0] = st[0]
        state_ref[1] = st[1]

    def finalize(acc, base, gr, outr):
        def body(c, carry):
            r0 = pl.multiple_of(c * CHUNK, CHUNK)
            o = acc[pl.ds(pl.multiple_of(base + r0, CHUNK), CHUNK), :]
            ms = jnp.mean(o * o, axis=-1, keepdims=True)
            y = o * lax.rsqrt(ms + RMS_EPS) * gr[pl.ds(r0, CHUNK), :].astype(F32)
            outr[pl.ds(r0, CHUNK), :] = y.astype(outr.dtype)
            return carry

        lax.fori_loop(0, outr.shape[0] // CHUNK, body, 0)

    ctx = (qc_ref, kc_ref, vc_ref, ac_ref)

    @pl.when(s == 0)
    def _():
        state_ref[...] = jnp.zeros_like(state_ref)
        acc_ref[...] = jnp.zeros_like(acc_ref)
        accc_ref[...] = jnp.zeros_like(accc_ref)
        scan_pair(ctx, ctx, accc_ref, 0, 0)

    @pl.when(s < nb)
    def _():
        scan_pair((qf_ref, kf_ref, vf_ref, af_ref), (qb_ref, kb_ref, vb_ref, ab_ref), acc_ref,
                  s * GLA_BLOCK, (nb - 1 - s) * GLA_BLOCK)

    @pl.when(s == nb)
    def _():
        finalize(accc_ref, 0, gc_ref, outc_ref)

    @pl.when(s >= nb)
    def _():
        finalize(acc_ref, (s - nb) * fin_rows, g_ref, out_ref)


def _gla_call(proj_l, alr_l, proj_c, alr_c, wup, bup, layer, batch, seq, ctx_len, d_model):
    d_key = d_model // 2
    head_k = d_key // N_HEADS
    head_v = d_model // N_HEADS
    nb = seq // GLA_BLOCK
    fin_rows = min(GLA_FINAL_ROWS, seq)
    nf = seq // fin_rows
    kq = d_key // head_k
    kv = (2 * d_key) // head_v
    kg = kv + d_model // head_v

    def blk(s, rev):
        return jnp.maximum(nb - 1 - s, 0) if rev else jnp.minimum(s, nb - 1)

    def lat(width, col0, rev):
        return pl.BlockSpec((GLA_BLOCK, width), lambda b, h, s: (b * nb + blk(s, rev), col0 + h))

    def lat_a(rev):
        return pl.BlockSpec((GLA_BLOCK, LANES), lambda b, h, s: (b * nb + blk(s, rev), 0))

    def fin(col0):
        return pl.BlockSpec((fin_rows, head_v),
                            lambda b, h, s: (b * nf + jnp.clip(s - nb, 0, nf - 1), col0 + h))

    def cx(width, col0):
        return pl.BlockSpec((ctx_len, width), lambda b, h, s: (b, col0 + h))

    in_specs = []
    for rev in (False, True):
        in_specs += [lat(head_k, 0, rev), lat(head_k, kq, rev), lat(head_v, kv, rev), lat_a(rev)]
    in_specs += [fin(kg),
                 cx(head_k, 0), cx(head_k, kq), cx(head_v, kv),
                 pl.BlockSpec((ctx_len, LANES), lambda b, h, s: (b, 0)),
                 cx(head_v, kg),
                 pl.BlockSpec((None, 2, LANES, head_k), lambda b, h, s: (layer, 0, 0, h)),
                 pl.BlockSpec((None, 2, 1, head_k), lambda b, h, s: (layer, 0, 0, h))]
    out_specs = [fin(0), cx(head_v, 0)]
    return pl.pallas_call(
        functools.partial(_gla_kernel, nb=nb),
        out_shape=(jax.ShapeDtypeStruct((batch * seq, d_model), BF16),
                   jax.ShapeDtypeStruct((batch * ctx_len, d_model), BF16)),
        grid=(batch, N_HEADS, nb + nf),
        in_specs=in_specs,
        out_specs=out_specs,
        scratch_shapes=[pltpu.VMEM((2, head_v, head_k), F32),
                        pltpu.VMEM((seq, head_v), F32),
                        pltpu.VMEM((ctx_len, head_v), F32)],
        compiler_params=_cparams(3),
        name="gla",
    )(proj_l, proj_l, proj_l, alr_l, proj_l, proj_l, proj_l, alr_l, proj_l,
      proj_c, proj_c, proj_c, alr_c, proj_c, wup, bup)


def _win(w):
    lo = w // 2
    return lo, w - lo - 1


def _band_matrix(n_tokens, period, w):
    lo, hi = _win(w)
    t = np.arange(n_tokens)
    same = (t[:, None] // period) == (t[None, :] // period)
    off = t[None, :] - t[:, None]
    return (same & (off >= -lo) & (off <= hi)).astype(np.float32)


def _count(idx, n, w):
    lo, hi = _win(w)
    return jnp.minimum(idx + hi + 1, n) - jnp.maximum(idx - lo, 0)


def _pool_lat_kernel(p_ref, band_ref, wg_ref, sc_ref, o_ref, ws_ref, *, n_rows):
    g = pl.program_id(1)
    seq = n_rows * GRID_W
    halo = 8 * GRID_W
    n_tiles = seq // POOL_TILE
    zeros = jnp.zeros((halo, ws_ref.shape[1]), F32)
    ws_ref[pl.ds(0, halo), :] = zeros
    ws_ref[pl.ds(halo + seq, halo), :] = zeros

    def wpass(t, carry):
        r0 = pl.multiple_of(t * POOL_TILE, POOL_TILE)
        ws_ref[pl.ds(halo + r0, POOL_TILE), :] = jnp.dot(
            band_ref[...], p_ref[pl.ds(r0, POOL_TILE), :], preferred_element_type=F32)
        return carry

    lax.fori_loop(0, n_tiles, wpass, 0)

    for gi, w in enumerate(POOL_WINDOWS):
        lo, hi = _win(w)

        @pl.when(g == gi)
        def _(lo=lo, hi=hi, w=w):
            def hpass(t, carry):
                r0 = pl.multiple_of(t * POOL_TILE, POOL_TILE)
                tot = ws_ref[pl.ds(halo + r0 - lo * GRID_W, POOL_TILE), :]
                for kk in range(-lo + 1, hi + 1):
                    tot = tot + ws_ref[pl.ds(halo + r0 + kk * GRID_W, POOL_TILE), :]
                tok = r0 + lax.broadcasted_iota(jnp.int32, (POOL_TILE, 1), 0)
                img_row = lax.shift_right_logical(tok, GRID_W.bit_length() - 1)
                img_col = jnp.bitwise_and(tok, GRID_W - 1)
                cnt = _count(img_row, n_rows, w) * _count(img_col, GRID_W, w)
                pin = p_ref[pl.ds(r0, POOL_TILE), :].astype(F32)
                diff = (tot / cnt.astype(F32) - pin).astype(BF16)
                y = jnp.dot(diff, wg_ref[...], preferred_element_type=F32) * sc_ref[...]
                o_ref[pl.ds(r0, POOL_TILE), :] = y.astype(o_ref.dtype)
                return carry

            lax.fori_loop(0, n_tiles, hpass, 0)


def _pool_lat_call(proj_l, band, wpg, pscale, layer, batch, seq, d_model):
    d_pool = d_model // 2
    grp = d_pool // len(POOL_WINDOWS)
    col0 = (d_model // 2 * 2 + 2 * d_model) // grp
    halo = 8 * GRID_W
    return pl.pallas_call(
        functools.partial(_pool_lat_kernel, n_rows=seq // GRID_W),
        out_shape=jax.ShapeDtypeStruct((batch * seq, d_pool), BF16),
        grid=(batch, len(POOL_WINDOWS)),
        in_specs=[pl.BlockSpec((seq, grp), lambda b, g: (b, col0 + g)),
                  pl.BlockSpec((None, POOL_TILE, POOL_TILE), lambda b, g: (g, 0, 0)),
                  pl.BlockSpec((None, None, grp, grp), lambda b, g: (layer, g, 0, 0)),
                  pl.BlockSpec((None, 1, grp), lambda b, g: (layer, 0, g))],
        out_specs=pl.BlockSpec((seq, grp), lambda b, g: (b, g)),
        scratch_shapes=[pltpu.VMEM((seq + 2 * halo, grp), F32)],
        compiler_params=_cparams(2),
        name="pool_lat",
    )(proj_l, band, wpg, pscale)


def _pool_ctx_kernel(p_ref, band_ref, icnt_ref, wg_ref, sc_ref, o_ref):
    tot = jnp.dot(band_ref[...], p_ref[...], preferred_element_type=F32)
    diff = (tot * icnt_ref[...] - p_ref[...].astype(F32)).astype(BF16)
    y = jnp.dot(diff, wg_ref[...], preferred_element_type=F32) * sc_ref[...]
    o_ref[...] = y.astype(o_ref.dtype)


def _pool_ctx_call(proj_c, band, icnt, wpg, pscale, layer, batch, ctx_len, d_model):
    d_pool = d_model // 2
    grp = d_pool // len(POOL_WINDOWS)
    col0 = (d_model // 2 * 2 + 2 * d_model) // grp
    return pl.pallas_call(
        _pool_ctx_kernel,
        out_shape=jax.ShapeDtypeStruct((batch * ctx_len, d_pool), BF16),
        grid=(batch, len(POOL_WINDOWS)),
        in_specs=[pl.BlockSpec((ctx_len, grp), lambda b, g: (b, col0 + g)),
                  pl.BlockSpec((None, ctx_len, ctx_len), lambda b, g: (g, 0, 0)),
                  pl.BlockSpec((None, ctx_len, 1), lambda b, g: (g, 0, 0)),
                  pl.BlockSpec((None, None, grp, grp), lambda b, g: (layer, g, 0, 0)),
                  pl.BlockSpec((None, 1, grp), lambda b, g: (layer, 0, g))],
        out_specs=pl.BlockSpec((ctx_len, grp), lambda b, g: (b, g)),
        compiler_params=_cparams(2),
        name="pool_ctx",
    )(proj_c, band, icnt, wpg, pscale)


def _merge_kernel(ga_ref, pa_ref, bgg_ref, bgp_ref, wg_ref, wp_ref, o_ref):
    yg = jnp.dot(ga_ref[...], wg_ref[...], preferred_element_type=F32)
    yp = jnp.dot(pa_ref[...], wp_ref[...], preferred_element_type=F32)
    y = (jax.nn.sigmoid(bgg_ref[...].astype(F32)) * yg
         + jax.nn.sigmoid(bgp_ref[...].astype(F32)) * yp)
    o_ref[...] = y.astype(o_ref.dtype)


def _merge_call(gla_act, pool_act, proj, w_gla_out, w_pool_out, layer, tm, tn, n_rows=None):
    m = gla_act.shape[0] if n_rows is None else n_rows
    d = w_gla_out.shape[-1]
    d_pool = pool_act.shape[1]
    bg0 = (proj.shape[1] - 2 * d) // tn
    return pl.pallas_call(
        _merge_kernel,
        out_shape=jax.ShapeDtypeStruct((m, d), BF16),
        grid=(d // tn, m // tm),
        in_specs=[pl.BlockSpec((tm, d), lambda j, i: (i, 0)),
                  pl.BlockSpec((tm, d_pool), lambda j, i: (i, 0)),
                  pl.BlockSpec((tm, tn), lambda j, i: (i, bg0 + j)),
                  pl.BlockSpec((tm, tn), lambda j, i: (i, bg0 + d // tn + j)),
                  pl.BlockSpec((None, d, tn), lambda j, i: (layer, 0, j)),
                  pl.BlockSpec((None, d_pool, tn), lambda j, i: (layer, 0, j))],
        out_specs=pl.BlockSpec((tm, tn), lambda j, i: (i, j)),
        compiler_params=_cparams(2),
        name="merge",
    )(gla_act, pool_act, proj, proj, w_gla_out, w_pool_out)


def _mm_ln_kernel(*refs, row_fn, nk, alpha, emit_h):
    if emit_h:
        (a_ref, w_ref, x_ref, gt_ref, lng_ref, lnb_ref, sc_ref, sh_ref, xo_ref, ho_ref, acc_ref) = refs
    else:
        (a_ref, w_ref, x_ref, gt_ref, lng_ref, lnb_ref, xo_ref, acc_ref) = refs
    kk = pl.program_id(1)
    part = jnp.dot(a_ref[...], w_ref[...], preferred_element_type=F32)

    @pl.when(kk == 0)
    def _():
        acc_ref[...] = part

    @pl.when(kk > 0)
    def _():
        acc_ref[...] += part

    @pl.when(kk == nk - 1)
    def _():
        r = row_fn(pl.program_id(0))
        y = alpha * x_ref[...] + gt_ref[pl.ds(r, 1), :] * acc_ref[...]
        mu = jnp.mean(y, axis=-1, keepdims=True)
        yc = y - mu
        var = jnp.mean(yc * yc, axis=-1, keepdims=True)
        xn = yc * lax.rsqrt(var + LN_EPS) * lng_ref[...] + lnb_ref[...]
        xo_ref[...] = xn
        if emit_h:
            ho_ref[...] = (xn * (1.0 + sc_ref[pl.ds(r, 1), :]) + sh_ref[pl.ds(r, 1), :]).astype(BF16)


def _mm_ln_call(a, w, x, mods, ln_gain, ln_bias, layer, gate_chunk, next_mod, row_fn, tm, tk, alpha,
                n_rows=None):
    m = x.shape[0] if n_rows is None else n_rows
    k = a.shape[1]
    d = x.shape[1]
    nk = k // tk
    emit_h = next_mod is not None
    in_specs = [pl.BlockSpec((tm, tk), lambda i, kk: (i, kk)),
                pl.BlockSpec((None, tk, d), lambda i, kk: (layer, kk, 0)),
                pl.BlockSpec((tm, d), lambda i, kk: (i, 0)),
                _mod_spec(layer, gate_chunk, d),
                pl.BlockSpec((None, 1, d), lambda i, kk: (layer, 0, 0)),
                pl.BlockSpec((None, 1, d), lambda i, kk: (layer, 0, 0))]
    args = [a, w, x, mods, ln_gain, ln_bias]
    out_shape = [jax.ShapeDtypeStruct((m, d), F32)]
    out_specs = [pl.BlockSpec((tm, d), lambda i, kk: (i, 0))]
    if emit_h:
        nl, sc_chunk, sh_chunk = next_mod
        in_specs += [_mod_spec(nl, sc_chunk, d), _mod_spec(nl, sh_chunk, d)]
        args += [mods, mods]
        out_shape.append(jax.ShapeDtypeStruct((m, d), BF16))
        out_specs.append(pl.BlockSpec((tm, d), lambda i, kk: (i, 0)))
    outs = pl.pallas_call(
        functools.partial(_mm_ln_kernel, row_fn=row_fn, nk=nk, alpha=alpha, emit_h=emit_h),
        out_shape=out_shape,
        grid=(m // tm, nk),
        in_specs=in_specs,
        out_specs=out_specs,
        scratch_shapes=[pltpu.VMEM((tm, d), F32)],
        compiler_params=_cparams(2),
        name="mm_ln",
    )(*args)
    return (outs[0], outs[1]) if emit_h else (outs[0], None)


def _ffn1_kernel(h_ref, wg_ref, wu_ref, o_ref):
    h = h_ref[...]
    gate = jnp.dot(h, wg_ref[...], preferred_element_type=F32)
    up = jnp.dot(h, wu_ref[...], preferred_element_type=F32)
    o_ref[...] = (_silu(gate) * up).astype(o_ref.dtype)


def _ffn1_call(h, w_ffn_in, layer, tm, tn, n_rows=None):
    m = h.shape[0] if n_rows is None else n_rows
    d = h.shape[1]
    d_ff = w_ffn_in.shape[-1] // 2
    nj = d_ff // tn
    return pl.pallas_call(
        _ffn1_kernel,
        out_shape=jax.ShapeDtypeStruct((m, d_ff), BF16),
        grid=(nj, m // tm),
        in_specs=[pl.BlockSpec((tm, d), lambda j, i: (i, 0)),
                  pl.BlockSpec((None, d, tn), lambda j, i: (layer, 0, j)),
                  pl.BlockSpec((None, d, tn), lambda j, i: (layer, 0, nj + j))],
        out_specs=pl.BlockSpec((tm, tn), lambda j, i: (i, j)),
        compiler_params=_cparams(2),
        name="ffn1",
    )(h, w_ffn_in, w_ffn_in)


def kernel(x, c, ctx, c_ctx, w_ada, b_ada, w_in, w_decay_up, b_decay_up, gla_norm_gain, w_pool_group, pool_scale, w_gla_out, w_pool_out, w_out, ln_mix_gain, ln_mix_bias, w_ffn_in, w_ffn_out, ln_ffn_gain, ln_ffn_bias):
    batch, seq, d = x.shape
    ctx_len = ctx.shape[1]
    depth = w_ada.shape[0]
    d_key = d // 2
    d_pool = d // 2
    d_ff = w_ffn_out.shape[1]
    alpha = (2.0 * depth) ** 0.25
    m_lat = batch * seq
    m_ctx = batch * ctx_len
    assert seq % (2 * GLA_BLOCK) == 0 and seq % LAT_TM == 0 and ctx_len % CHUNK == 0
    assert batch + 1 <= 8

    x_l = x.reshape(m_lat, d)
    x_c = ctx.reshape(m_ctx, d)
    cond = jnp.zeros((8, d), F32).at[:batch].set(c).at[batch].set(c_ctx)
    a0 = 2 * d_key + 2 * d
    a1 = a0 + 2 * GATE_RANK
    w_main = jnp.concatenate([w_in[:, :, :a0], w_in[:, :, a1:]], axis=-1).astype(BF16)
    w_alr = jnp.pad(w_in[:, :, a0:a1], ((0, 0), (0, 0), (0, LANES - 2 * GATE_RANK))).astype(BF16)
    wup = jnp.zeros((depth, 2, LANES, d_key), F32)
    wup = wup.at[:, 0, :GATE_RANK].set(w_decay_up[:, 0]).at[:, 1, GATE_RANK:2 * GATE_RANK].set(w_decay_up[:, 1])
    wup = wup.astype(BF16)
    bup = b_decay_up.reshape(depth, 2, 1, d_key)
    gain = gla_norm_gain.reshape(depth, 1, d)
    wpg = w_pool_group.astype(BF16)
    pscale = pool_scale.reshape(depth, 1, d_pool)
    w_gla_out_b = w_gla_out.astype(BF16)
    w_pool_out_b = w_pool_out.astype(BF16)
    w_out_b = w_out.astype(BF16)
    w_ffn_in_b = w_ffn_in.astype(BF16)
    w_ffn_out_b = w_ffn_out.astype(BF16)
    lnm_g = ln_mix_gain.reshape(depth, 1, d)
    lnm_b = ln_mix_bias.reshape(depth, 1, d)
    lnf_g = ln_ffn_gain.reshape(depth, 1, d)
    lnf_b = ln_ffn_bias.reshape(depth, 1, d)
    band_lat = jnp.asarray(np.stack([_band_matrix(POOL_TILE, GRID_W, w) for w in POOL_WINDOWS]), BF16)
    band_ctx = jnp.asarray(np.stack([_band_matrix(ctx_len, ctx_len, w) for w in POOL_WINDOWS]), BF16)
    t = np.arange(ctx_len)
    icnt_ctx = jnp.asarray(np.stack([
        1.0 / (np.minimum(t + _win(w)[1] + 1, ctx_len) - np.maximum(t - _win(w)[0], 0))
        for w in POOL_WINDOWS]).astype(np.float32)[:, :, None])

    mods = _mods_call(cond, w_ada, b_ada)

    lat_row_1024 = lambda i: i // (seq // LAT_TM)
    lat_row_512 = lambda i: i // (seq // LN_TM)
    ctx_row = lambda i: batch

    h_l = _modulate_call(x_l, mods, 0, lat_row_1024, LAT_TM)
    h_c = _modulate_call(x_c, mods, 0, ctx_row, m_ctx)

    ffn_tk = d_ff // 4
    for layer in range(depth):
        ctx_out = layer < depth - 1
        proj_l = _in_proj_call(h_l, w_main, gain, layer, LAT_TM, 1024, "in_proj")
        alr_l = _mm_call(h_l, w_alr, layer, LAT_TM, LANES, F32, "in_proj_decay")
        proj_c = _in_proj_call(h_c, w_main, gain, layer, m_ctx, 1024, "in_proj_ctx")
        alr_c = _mm_call(h_c, w_alr, layer, m_ctx, LANES, F32, "in_proj_decay_ctx")
        gla_l, gla_c = _gla_call(proj_l, alr_l, proj_c, alr_c, wup, bup, layer,
                                 batch, seq, ctx_len, d)
        pool_l = _pool_lat_call(proj_l, band_lat, wpg, pscale, layer, batch, seq, d)
        merged_l = _merge_call(gla_l, pool_l, proj_l, w_gla_out_b, w_pool_out_b, layer, LAT_TM, 1024)
        x_l, hf_l = _mm_ln_call(merged_l, w_out_b, x_l, mods, lnm_g, lnm_b, layer, 2,
                                (layer, 4, 3), lat_row_512, LN_TM, d, alpha)
        act_l = _ffn1_call(hf_l, w_ffn_in_b, layer, LAT_TM, 512)
        nxt = (layer + 1, 1, 0) if ctx_out else None
        x_l, h_l = _mm_ln_call(act_l, w_ffn_out_b, x_l, mods, lnf_g, lnf_b, layer, 5,
                               nxt, lat_row_512, LN_TM, ffn_tk, alpha)
        if ctx_out:
            pool_c = _pool_ctx_call(proj_c, band_ctx, icnt_ctx, wpg, pscale, layer, batch, ctx_len, d)
            merged_c = _merge_call(gla_c, pool_c, proj_c, w_gla_out_b, w_pool_out_b, layer, m_ctx, 1024)
            x_c, hf_c = _mm_ln_call(merged_c, w_out_b, x_c, mods, lnm_g, lnm_b, layer, 2,
                                    (layer, 4, 3), ctx_row, m_ctx, d, alpha)
            act_c = _ffn1_call(hf_c, w_ffn_in_b, layer, m_ctx, 512)
            x_c, h_c = _mm_ln_call(act_c, w_ffn_out_b, x_c, mods, lnf_g, lnf_b, layer, 5,
                                   (layer + 1, 1, 0), ctx_row, m_ctx, ffn_tk, alpha)
    return x_l.reshape(batch, seq, d)
```

```python
import functools

import numpy as np
import jax
import jax.numpy as jnp
from jax import lax
from jax.experimental import pallas as pl
from jax.experimental.pallas import tpu as pltpu

F32 = jnp.float32
BF16 = jnp.bfloat16

GRID_W = 64
N_HEADS = 4
GATE_RANK = 16
GATE_NORM = 16.0
CHUNK = 64
POOL_WINDOWS = (2, 4, 8, 16)
N_MOD = 6
LN_EPS = 1e-5
RMS_EPS = 1e-6
LOG2E = 1.4426950408889634

LANES = 128
VMEM_LIMIT = 56 * 1024 * 1024

GLA_BLOCK = 512
GLA_FINAL_ROWS = 2048
POOL_TILE = 256
POOL_HALO_TILES = 2
LAT_TM = 1024
LN_TM = 512
LN_ROWS = 16

def _cparams(n_axes, vmem=VMEM_LIMIT):
    return pltpu.CompilerParams(dimension_semantics=("arbitrary",) * n_axes,
                                vmem_limit_bytes=vmem)


def _silu(v):
    return v * jax.nn.sigmoid(v)


def _mods_kernel(cond_ref, w_ref, b_ref, o_ref):
    a = _silu(cond_ref[...]).astype(BF16)
    o_ref[...] = jnp.dot(a, w_ref[...].astype(BF16), preferred_element_type=F32) + b_ref[...]


def _mods_call(cond, w_ada, b_ada):
    depth, d, n = w_ada.shape
    tn = 1024
    return pl.pallas_call(
        _mods_kernel,
        out_shape=jax.ShapeDtypeStruct((depth, 8, n), F32),
        grid=(depth, n // tn),
        in_specs=[pl.BlockSpec((8, d), lambda l, j: (0, 0)),
                  pl.BlockSpec((None, d, tn), lambda l, j: (l, 0, j)),
                  pl.BlockSpec((None, 1, tn), lambda l, j: (l, 0, j))],
        out_specs=pl.BlockSpec((None, 8, tn), lambda l, j: (l, 0, j)),
        compiler_params=_cparams(2),
        name="mods",
    )(cond, w_ada, b_ada.reshape(depth, 1, n))


def _mod_spec(layer, chunk, d):
    return pl.BlockSpec((None, 8, d), lambda *_: (layer, 0, chunk))


def _modulate_kernel(x_ref, sc_ref, sh_ref, o_ref, *, row_fn):
    r = row_fn(pl.program_id(0))
    sc = sc_ref[pl.ds(r, 1), :]
    sh = sh_ref[pl.ds(r, 1), :]
    o_ref[...] = (x_ref[...] * (1.0 + sc) + sh).astype(BF16)


def _modulate_call(x, mods, layer, row_fn, tm):
    m, d = x.shape
    return pl.pallas_call(
        functools.partial(_modulate_kernel, row_fn=row_fn),
        out_shape=jax.ShapeDtypeStruct((m, d), BF16),
        grid=(m // tm,),
        in_specs=[pl.BlockSpec((tm, d), lambda i: (i, 0)),
                  _mod_spec(layer, 1, d), _mod_spec(layer, 0, d)],
        out_specs=pl.BlockSpec((tm, d), lambda i: (i, 0)),
        compiler_params=_cparams(1),
        name="modulate",
    )(x, mods, mods)


def _mm_kernel(a_ref, w_ref, o_ref):
    o_ref[...] = jnp.dot(a_ref[...], w_ref[...], preferred_element_type=F32).astype(o_ref.dtype)


def _mm_call(a, w, layer, tm, tn, out_dtype, name):
    m, k = a.shape
    n = w.shape[-1]
    return pl.pallas_call(
        _mm_kernel,
        out_shape=jax.ShapeDtypeStruct((m, n), out_dtype),
        grid=(n // tn, m // tm),
        in_specs=[pl.BlockSpec((tm, k), lambda j, i: (i, 0)),
                  pl.BlockSpec((None, k, tn), lambda j, i: (layer, 0, j))],
        out_specs=pl.BlockSpec((tm, tn), lambda j, i: (i, j)),
        compiler_params=_cparams(2),
        name=name,
    )(a, w)


def _in_proj_kernel(a_ref, wa_ref, wc_ref, gain_ref, o_ref, wb_ref, *, n_q, g_lo, g_hi, n_a, q_scale):
    j = pl.program_id(0)
    is_q = j < n_q
    is_g = jnp.logical_and(j >= g_lo, j < g_hi)
    is_c = j >= n_a

    @pl.when(jnp.logical_and(pl.program_id(1) == 0, j < n_a))
    def _():
        wb_ref[...] = wa_ref[...].astype(BF16)

    def mm(w_ref):
        return jnp.dot(a_ref[...], w_ref[...], preferred_element_type=F32)

    @pl.when(is_q)
    def _():
        o_ref[...] = (mm(wb_ref) * q_scale).astype(o_ref.dtype)

    @pl.when(is_g)
    def _():
        o_ref[...] = (_silu(mm(wb_ref)) * gain_ref[...]).astype(o_ref.dtype)

    @pl.when(jnp.logical_not(jnp.logical_or(jnp.logical_or(is_q, is_g), is_c)))
    def _():
        o_ref[...] = mm(wb_ref).astype(o_ref.dtype)

    @pl.when(is_c)
    def _():
        o_ref[...] = mm(wc_ref).astype(o_ref.dtype)


def _in_proj_call(a, w_in, w_tail, gain, layer, tm, tn, name):
    m, k = a.shape
    d = gain.shape[-1]
    d_key = d // 2
    n_head = 2 * d_key + 2 * d
    n = n_head + w_tail.shape[-1]
    n_a = n_head // tn
    g_lo = (2 * d_key + d) // tn
    g_hi = g_lo + d // tn
    kern = functools.partial(_in_proj_kernel, n_q=d_key // tn, g_lo=g_lo, g_hi=g_hi, n_a=n_a,
                             q_scale=(d_key // N_HEADS) ** -0.5)
    return pl.pallas_call(
        kern,
        out_shape=jax.ShapeDtypeStruct((m, n), BF16),
        grid=(n // tn, m // tm),
        in_specs=[pl.BlockSpec((tm, k), lambda j, i: (i, 0)),
                  pl.BlockSpec((None, k, tn), lambda j, i: (layer, 0, jnp.minimum(j, n_a - 1))),
                  pl.BlockSpec((None, k, tn), lambda j, i: (layer, 0, jnp.maximum(j - n_a, 0))),
                  pl.BlockSpec((None, 1, tn),
                               lambda j, i: (layer, 0, jnp.clip(j - g_lo, 0, g_hi - g_lo - 1)))],
        out_specs=pl.BlockSpec((tm, tn), lambda j, i: (i, j)),
        scratch_shapes=[pltpu.VMEM((k, tn), BF16)],
        compiler_params=_cparams(2),
        name=name,
    )(a, w_in, w_tail, gain)


def _gla_kernel(qf_ref, kf_ref, vf_ref, af_ref, qb_ref, kb_ref, vb_ref, ab_ref, g_ref,
                qc_ref, kc_ref, vc_ref, ac_ref, gc_ref,
                wup_ref, bup_ref,
                out_ref, outc_ref,
                state_ref, acc_ref, accc_ref, *, nb):
    s = pl.program_id(2)
    fin_rows = out_ref.shape[0]

    row = lax.broadcasted_iota(jnp.int32, (CHUNK, CHUNK), 0)
    col = lax.broadcasted_iota(jnp.int32, (CHUNK, CHUNK), 1)
    keep = (row >= col, row <= col)
    ref_row = (CHUNK // 2 - 1, CHUNK // 2)
    last_row = (CHUNK - 1, 0)

    def log_decay(d, ar):
        z = jnp.dot(ar[...].astype(BF16), wup_ref[d], preferred_element_type=F32) + bup_ref[d]
        log_a = ((jnp.minimum(z, 0.0) - jnp.log(1.0 + jnp.exp2(jnp.abs(z) * (-LOG2E))))
                 * (LOG2E / GATE_NORM))
        hi = log_a.astype(BF16)
        return hi, (log_a - hi.astype(F32)).astype(BF16)

    def intra(d, tri, hi, lo, qr, kr, vr, c):
        sl = slice(c * CHUNK, (c + 1) * CHUNK)
        cum = (jnp.dot(tri, hi[sl], preferred_element_type=F32)
               + jnp.dot(tri, lo[sl], preferred_element_type=F32))
        ref = cum[ref_row[d]:ref_row[d] + 1, :]
        last = cum[last_row[d]:last_row[d] + 1, :]
        q = qr[sl, :].astype(F32)
        k = kr[sl, :].astype(F32)
        v = vr[sl, :]
        q_in = (q * jnp.exp2(cum - ref)).astype(BF16)
        k_in = (k * jnp.exp2(ref - cum)).astype(BF16)
        scores = lax.dot_general(q_in, k_in, (((1,), (1,)), ((), ())), preferred_element_type=F32)
        scores = jnp.where(keep[d], scores, 0.0).astype(BF16)
        o_intra = jnp.dot(scores, v, preferred_element_type=F32)
        q_inter = (q * jnp.exp2(cum)).astype(BF16)
        k_state = (k * jnp.exp2(last - cum)).astype(BF16)
        upd = lax.dot_general(v, k_state, (((0,), (0,)), ((), ())), preferred_element_type=F32)
        return o_intra, q_inter, upd, jnp.exp2(last)

    def scan_pair(fwd, bwd, acc, off_f, off_b):
        streams = []
        for d, (qr, kr, vr, ar), off in ((0, fwd, off_f), (1, bwd, off_b)):
            n_chunks = qr.shape[0] // CHUNK
            order = list(range(n_chunks - 1, -1, -1) if d == 1 else range(n_chunks))
            hi, lo = log_decay(d, ar)
            tri = jnp.where(keep[d], 1.0, 0.0).astype(BF16)
            streams.append((d, tri, qr, kr, vr, hi, lo, order, off))
        n_steps = len(streams[0][7])
        pre = {}
        for i in range(n_steps):
            for d, tri, qr, kr, vr, hi, lo, order, _ in streams:
                pre[d, i] = intra(d, tri, hi, lo, qr, kr, vr, order[i])
        st = [state_ref[0], state_ref[1]]
        for i in range(n_steps):
            for d, _, _, _, _, _, _, order, off in streams:
                o_intra, q_inter, upd, decay = pre[d, i]
                o = o_intra + lax.dot_general(q_inter, st[d].astype(BF16), (((1,), (1,)), ((), ())),
                                              preferred_element_type=F32)
                st[d] = st[d] * decay + upd
                ra = pl.multiple_of(off + order[i] * CHUNK, CHUNK)
                acc[pl.ds(ra, CHUNK), :] += o
        state_ref[0] = st[0]
        state_ref[1] = st[1]

    def finalize(acc, base, gr, outr):
        def body(c, carry):
            r0 = pl.multiple_of(c * CHUNK, CHUNK)
            o = acc[pl.ds(pl.multiple_of(base + r0, CHUNK), CHUNK), :]
            ms = jnp.mean(o * o, axis=-1, keepdims=True)
            y = o * lax.rsqrt(ms + RMS_EPS) * gr[pl.ds(r0, CHUNK), :].astype(F32)
            outr[pl.ds(r0, CHUNK), :] = y.astype(outr.dtype)
            return carry

        lax.fori_loop(0, outr.shape[0] // CHUNK, body, 0)

    ctx = (qc_ref, kc_ref, vc_ref, ac_ref)

    @pl.when(s == 0)
    def _():
        state_ref[...] = jnp.zeros_like(state_ref)
        acc_ref[...] = jnp.zeros_like(acc_ref)
        accc_ref[...] = jnp.zeros_like(accc_ref)
        scan_pair(ctx, ctx, accc_ref, 0, 0)

    @pl.when(s < nb)
    def _():
        scan_pair((qf_ref, kf_ref, vf_ref, af_ref), (qb_ref, kb_ref, vb_ref, ab_ref), acc_ref,
                  s * GLA_BLOCK, (nb - 1 - s) * GLA_BLOCK)

    @pl.when(s == nb)
    def _():
        finalize(accc_ref, 0, gc_ref, outc_ref)

    @pl.when(s >= nb)
    def _():
        finalize(acc_ref, (s - nb) * fin_rows, g_ref, out_ref)


def _gla_call(proj_l, alr_l, proj_c, alr_c, wup, bup, layer, batch, seq, ctx_len, d_model):
    d_key = d_model // 2
    head_k = d_key // N_HEADS
    head_v = d_model // N_HEADS
    nb = seq // GLA_BLOCK
    fin_rows = min(GLA_FINAL_ROWS, seq)
    nf = seq // fin_rows
    kq = d_key // head_k
    kv = (2 * d_key) // head_v
    kg = kv + d_model // head_v

    def blk(s, rev):
        return jnp.maximum(nb - 1 - s, 0) if rev else jnp.minimum(s, nb - 1)

    def lat(width, col0, rev):
        return pl.BlockSpec((GLA_BLOCK, width), lambda b, h, s: (b * nb + blk(s, rev), col0 + h))

    def lat_a(rev):
        return pl.BlockSpec((GLA_BLOCK, LANES), lambda b, h, s: (b * nb + blk(s, rev), 0))

    def fin(col0):
        return pl.BlockSpec((fin_rows, head_v),
                            lambda b, h, s: (b * nf + jnp.clip(s - nb, 0, nf - 1), col0 + h))

    def cx(width, col0):
        return pl.BlockSpec((ctx_len, width), lambda b, h, s: (b, col0 + h))

    in_specs = []
    for rev in (False, True):
        in_specs += [lat(head_k, 0, rev), lat(head_k, kq, rev), lat(head_v, kv, rev), lat_a(rev)]
    in_specs += [fin(kg),
                 cx(head_k, 0), cx(head_k, kq), cx(head_v, kv),
                 pl.BlockSpec((ctx_len, LANES), lambda b, h, s: (b, 0)),
                 cx(head_v, kg),
                 pl.BlockSpec((None, 2, LANES, head_k), lambda b, h, s: (layer, 0, 0, h)),
                 pl.BlockSpec((None, 2, 1, head_k), lambda b, h, s: (layer, 0, 0, h))]
    out_specs = [fin(0), cx(head_v, 0)]
    return pl.pallas_call(
        functools.partial(_gla_kernel, nb=nb),
        out_shape=(jax.ShapeDtypeStruct((batch * seq, d_model), BF16),
                   jax.ShapeDtypeStruct((batch * ctx_len, d_model), BF16)),
        grid=(batch, N_HEADS, nb + nf),
        in_specs=in_specs,
        out_specs=out_specs,
        scratch_shapes=[pltpu.VMEM((2, head_v, head_k), F32),
                        pltpu.VMEM((seq, head_v), F32),
                        pltpu.VMEM((ctx_len, head_v), F32)],
        compiler_params=_cparams(3),
        name="gla",
    )(proj_l, proj_l, proj_l, alr_l, proj_l, proj_l, proj_l, alr_l, proj_l,
      proj_c, proj_c, proj_c, alr_c, proj_c, wup, bup)


def _win(w):
    lo = w // 2
    return lo, w - lo - 1


def _band_matrix(n_tokens, w):
    lo, hi = _win(w)
    t = np.arange(n_tokens)
    off = t[None, :] - t[:, None]
    return ((off >= -lo) & (off <= hi)).astype(np.float32)


def _box_offsets(w):
    lo, hi = _win(w)
    rpt = POOL_TILE // GRID_W
    return list(range(-((lo + rpt - 1) // rpt), (hi + rpt - 1) // rpt + 1))


def _box_matrix(w, tile_offset):
    lo, hi = _win(w)
    rpt = POOL_TILE // GRID_W
    t = np.arange(POOL_TILE)
    a, c = t // GRID_W, t % GRID_W
    drow = rpt * tile_offset + a[None, :] - a[:, None]
    dcol = c[None, :] - c[:, None]
    return ((drow >= -lo) & (drow <= hi) & (dcol >= -lo) & (dcol <= hi)).astype(np.float32)


def _count(idx, n, w):
    lo, hi = _win(w)
    return jnp.minimum(idx + hi + 1, n) - jnp.maximum(idx - lo, 0)


def _pool_lat_kernel(p_ref, box_ref, wg_ref, sc_ref, o_ref, ps_ref, *, n_rows):
    g = pl.program_id(1)
    seq = n_rows * GRID_W
    halo = POOL_HALO_TILES * POOL_TILE
    n_tiles = seq // POOL_TILE
    zeros = jnp.zeros((halo, ps_ref.shape[1]), ps_ref.dtype)
    ps_ref[pl.ds(0, halo), :] = zeros
    ps_ref[pl.ds(halo + seq, halo), :] = zeros
    ps_ref[pl.ds(halo, seq), :] = p_ref[...]

    for gi, w in enumerate(POOL_WINDOWS):
        offsets = _box_offsets(w)

        @pl.when(g == gi)
        def _(w=w, offsets=offsets):
            def tile(t, carry):
                r0 = pl.multiple_of(t * POOL_TILE, POOL_TILE)
                tot = None
                for j, off in enumerate(offsets):
                    src = ps_ref[pl.ds(halo + r0 + off * POOL_TILE, POOL_TILE), :]
                    part = jnp.dot(box_ref[j], src, preferred_element_type=F32)
                    tot = part if tot is None else tot + part
                tok = r0 + lax.broadcasted_iota(jnp.int32, (POOL_TILE, 1), 0)
                img_row = lax.shift_right_logical(tok, GRID_W.bit_length() - 1)
                img_col = jnp.bitwise_and(tok, GRID_W - 1)
                cnt = _count(img_row, n_rows, w) * _count(img_col, GRID_W, w)
                pin = p_ref[pl.ds(r0, POOL_TILE), :].astype(F32)
                diff = (tot * (1.0 / cnt.astype(F32)) - pin).astype(BF16)
                y = jnp.dot(diff, wg_ref[...], preferred_element_type=F32) * sc_ref[...]
                o_ref[pl.ds(r0, POOL_TILE), :] = y.astype(o_ref.dtype)
                return carry

            lax.fori_loop(0, n_tiles, tile, 0, unroll=4)


def _pool_lat_call(proj_l, box, wpg, pscale, layer, batch, seq, d_model):
    d_pool = d_model // 2
    grp = d_pool // len(POOL_WINDOWS)
    col0 = (d_model // 2 * 2 + 2 * d_model) // grp
    halo = POOL_HALO_TILES * POOL_TILE
    return pl.pallas_call(
        functools.partial(_pool_lat_kernel, n_rows=seq // GRID_W),
        out_shape=jax.ShapeDtypeStruct((batch * seq, d_pool), BF16),
        grid=(batch, len(POOL_WINDOWS)),
        in_specs=[pl.BlockSpec((seq, grp), lambda b, g: (b, col0 + g)),
                  pl.BlockSpec((None, box.shape[1], POOL_TILE, POOL_TILE), lambda b, g: (g, 0, 0, 0)),
                  pl.BlockSpec((None, None, grp, grp), lambda b, g: (layer, g, 0, 0)),
                  pl.BlockSpec((None, 1, grp), lambda b, g: (layer, 0, g))],
        out_specs=pl.BlockSpec((seq, grp), lambda b, g: (b, g)),
        scratch_shapes=[pltpu.VMEM((seq + 2 * halo, grp), BF16)],
        compiler_params=_cparams(2),
        name="pool_lat",
    )(proj_l, box, wpg, pscale)


def _pool_ctx_kernel(p_ref, band_ref, icnt_ref, wg_ref, sc_ref, o_ref):
    tot = jnp.dot(band_ref[...], p_ref[...], preferred_element_type=F32)
    diff = (tot * icnt_ref[...] - p_ref[...].astype(F32)).astype(BF16)
    y = jnp.dot(diff, wg_ref[...], preferred_element_type=F32) * sc_ref[...]
    o_ref[...] = y.astype(o_ref.dtype)


def _pool_ctx_call(proj_c, band, icnt, wpg, pscale, layer, batch, ctx_len, d_model):
    d_pool = d_model // 2
    grp = d_pool // len(POOL_WINDOWS)
    col0 = (d_model // 2 * 2 + 2 * d_model) // grp
    return pl.pallas_call(
        _pool_ctx_kernel,
        out_shape=jax.ShapeDtypeStruct((batch * ctx_len, d_pool), BF16),
        grid=(batch, len(POOL_WINDOWS)),
        in_specs=[pl.BlockSpec((ctx_len, grp), lambda b, g: (b, col0 + g)),
                  pl.BlockSpec((None, ctx_len, ctx_len), lambda b, g: (g, 0, 0)),
                  pl.BlockSpec((None, ctx_len, 1), lambda b, g: (g, 0, 0)),
                  pl.BlockSpec((None, None, grp, grp), lambda b, g: (layer, g, 0, 0)),
                  pl.BlockSpec((None, 1, grp), lambda b, g: (layer, 0, g))],
        out_specs=pl.BlockSpec((ctx_len, grp), lambda b, g: (b, g)),
        compiler_params=_cparams(2),
        name="pool_ctx",
    )(proj_c, band, icnt, wpg, pscale)


def _merge_kernel(ga_ref, pa_ref, bgg_ref, bgp_ref, wg_ref, wp_ref, o_ref, wgb_ref, wpb_ref):
    @pl.when(pl.program_id(1) == 0)
    def _():
        wgb_ref[...] = wg_ref[...].astype(BF16)
        wpb_ref[...] = wp_ref[...].astype(BF16)

    yg = jnp.dot(ga_ref[...], wgb_ref[...], preferred_element_type=F32)
    yp = jnp.dot(pa_ref[...], wpb_ref[...], preferred_element_type=F32)
    y = (jax.nn.sigmoid(bgg_ref[...].astype(F32)) * yg
         + jax.nn.sigmoid(bgp_ref[...].astype(F32)) * yp)
    o_ref[...] = y.astype(o_ref.dtype)


def _merge_call(gla_act, pool_act, proj, w_gla_out, w_pool_out, layer, tm, tn, n_rows=None):
    m = gla_act.shape[0] if n_rows is None else n_rows
    d = w_gla_out.shape[-1]
    d_pool = pool_act.shape[1]
    bg0 = (proj.shape[1] - 2 * d) // tn
    return pl.pallas_call(
        _merge_kernel,
        out_shape=jax.ShapeDtypeStruct((m, d), BF16),
        grid=(d // tn, m // tm),
        in_specs=[pl.BlockSpec((tm, d), lambda j, i: (i, 0)),
                  pl.BlockSpec((tm, d_pool), lambda j, i: (i, 0)),
                  pl.BlockSpec((tm, tn), lambda j, i: (i, bg0 + j)),
                  pl.BlockSpec((tm, tn), lambda j, i: (i, bg0 + d // tn + j)),
                  pl.BlockSpec((None, d, tn), lambda j, i: (layer, 0, j)),
                  pl.BlockSpec((None, d_pool, tn), lambda j, i: (layer, 0, j))],
        out_specs=pl.BlockSpec((tm, tn), lambda j, i: (i, j)),
        scratch_shapes=[pltpu.VMEM((d, tn), BF16), pltpu.VMEM((d_pool, tn), BF16)],
        compiler_params=_cparams(2),
        name="merge",
    )(gla_act, pool_act, proj, proj, w_gla_out, w_pool_out)


def _mm_ln_kernel(*refs, row_fn, nm, nk, alpha, emit_h):
    if emit_h:
        (a_ref, w_ref, x_ref, gt_ref, lng_ref, lnb_ref, sc_ref, sh_ref, xo_ref, ho_ref, *acc) = refs
    else:
        (a_ref, w_ref, x_ref, gt_ref, lng_ref, lnb_ref, xo_ref, *acc) = refs
    i = pl.program_id(0)
    kk = pl.program_id(1)
    tq = x_ref.shape[0] // nk

    def matmul(acc_ref):
        part = jnp.dot(a_ref[...], w_ref[...], preferred_element_type=F32)
        if nk == 1:
            acc_ref[...] = part
        else:
            acc_ref[...] += part

    def epilogue(acc_ref):
        base = kk * tq
        r = row_fn(i - 1)
        gt = gt_ref[pl.ds(r, 1), :]
        lng = lng_ref[...]
        lnb = lnb_ref[...]
        if emit_h:
            sc1 = 1.0 + sc_ref[pl.ds(r, 1), :]
            sh = sh_ref[pl.ds(r, 1), :]
        for c in range(tq // LN_ROWS):
            rows = pl.ds(pl.multiple_of(base + c * LN_ROWS, LN_ROWS), LN_ROWS)
            y = alpha * x_ref[rows, :] + gt * acc_ref[rows, :]
            mu = jnp.mean(y, axis=-1, keepdims=True)
            yc = y - mu
            var = jnp.mean(yc * yc, axis=-1, keepdims=True)
            xn = yc * lax.rsqrt(var + LN_EPS) * lng + lnb
            xo_ref[rows, :] = xn
            if emit_h:
                ho_ref[rows, :] = (xn * sc1 + sh).astype(BF16)
            if nk > 1:
                acc_ref[rows, :] = jnp.zeros((LN_ROWS, acc_ref.shape[1]), F32)

    @pl.when(i == 0)
    def _():
        if nk > 1:
            @pl.when(kk == 0)
            def _():
                acc[0][...] = jnp.zeros_like(acc[0])
                acc[1][...] = jnp.zeros_like(acc[1])
        matmul(acc[0])

    for parity in (0, 1):
        @pl.when(jnp.logical_and(jnp.logical_and(i >= 1, i < nm), i % 2 == parity))
        def _(parity=parity):
            epilogue(acc[1 - parity])
            matmul(acc[parity])

    @pl.when(i == nm)
    def _():
        epilogue(acc[(nm - 1) % 2])


def _mm_ln_call(a, w, x, mods, ln_gain, ln_bias, layer, gate_chunk, next_mod, row_fn, tm, tk, alpha,
                n_rows=None):
    m = x.shape[0] if n_rows is None else n_rows
    k = a.shape[1]
    d = x.shape[1]
    nk = k // tk
    nm = m // tm
    emit_h = next_mod is not None

    def k_idx(i, kk):
        return jnp.where(i < nm, kk, nk - 1)

    prev = lambda i, kk: (jnp.maximum(i - 1, 0), 0)
    in_specs = [pl.BlockSpec((tm, tk), lambda i, kk: (jnp.minimum(i, nm - 1), k_idx(i, kk))),
                pl.BlockSpec((None, tk, d), lambda i, kk: (layer, k_idx(i, kk), 0)),
                pl.BlockSpec((tm, d), prev),
                _mod_spec(layer, gate_chunk, d),
                pl.BlockSpec((None, 1, d), lambda i, kk: (layer, 0, 0)),
                pl.BlockSpec((None, 1, d), lambda i, kk: (layer, 0, 0))]
    args = [a, w, x, mods, ln_gain, ln_bias]
    out_shape = [jax.ShapeDtypeStruct((m, d), F32)]
    out_specs = [pl.BlockSpec((tm, d), prev)]
    if emit_h:
        nl, sc_chunk, sh_chunk = next_mod
        in_specs += [_mod_spec(nl, sc_chunk, d), _mod_spec(nl, sh_chunk, d)]
        args += [mods, mods]
        out_shape.append(jax.ShapeDtypeStruct((m, d), BF16))
        out_specs.append(pl.BlockSpec((tm, d), prev))
    outs = pl.pallas_call(
        functools.partial(_mm_ln_kernel, row_fn=row_fn, nm=nm, nk=nk, alpha=alpha, emit_h=emit_h),
        out_shape=out_shape,
        grid=(nm + 1, nk),
        in_specs=in_specs,
        out_specs=out_specs,
        scratch_shapes=[pltpu.VMEM((tm, d), F32), pltpu.VMEM((tm, d), F32)],
        compiler_params=_cparams(2),
        name="mm_ln",
    )(*args)
    return (outs[0], outs[1]) if emit_h else (outs[0], None)


def _ffn1_kernel(h_ref, wg_ref, wu_ref, o_ref, wgb_ref, wub_ref):
    @pl.when(pl.program_id(1) == 0)
    def _():
        wgb_ref[...] = wg_ref[...].astype(BF16)
        wub_ref[...] = wu_ref[...].astype(BF16)

    h = h_ref[...]
    gate = jnp.dot(h, wgb_ref[...], preferred_element_type=F32)
    up = jnp.dot(h, wub_ref[...], preferred_element_type=F32)
    o_ref[...] = (_silu(gate) * up).astype(o_ref.dtype)


def _ffn1_call(h, w_ffn_in, layer, tm, tn, n_rows=None):
    m = h.shape[0] if n_rows is None else n_rows
    d = h.shape[1]
    d_ff = w_ffn_in.shape[-1] // 2
    nj = d_ff // tn
    return pl.pallas_call(
        _ffn1_kernel,
        out_shape=jax.ShapeDtypeStruct((m, d_ff), BF16),
        grid=(nj, m // tm),
        in_specs=[pl.BlockSpec((tm, d), lambda j, i: (i, 0)),
                  pl.BlockSpec((None, d, tn), lambda j, i: (layer, 0, j)),
                  pl.BlockSpec((None, d, tn), lambda j, i: (layer, 0, nj + j))],
        out_specs=pl.BlockSpec((tm, tn), lambda j, i: (i, j)),
        scratch_shapes=[pltpu.VMEM((d, tn), BF16), pltpu.VMEM((d, tn), BF16)],
        compiler_params=_cparams(2),
        name="ffn1",
    )(h, w_ffn_in, w_ffn_in)


def kernel(x, c, ctx, c_ctx, w_ada, b_ada, w_in, w_decay_up, b_decay_up, gla_norm_gain, w_pool_group, pool_scale, w_gla_out, w_pool_out, w_out, ln_mix_gain, ln_mix_bias, w_ffn_in, w_ffn_out, ln_ffn_gain, ln_ffn_bias):
    batch, seq, d = x.shape
    ctx_len = ctx.shape[1]
    depth = w_ada.shape[0]
    d_key = d // 2
    d_pool = d // 2
    d_ff = w_ffn_out.shape[1]
    alpha = (2.0 * depth) ** 0.25
    m_lat = batch * seq
    m_ctx = batch * ctx_len
    assert seq % (2 * GLA_BLOCK) == 0 and seq % LAT_TM == 0 and ctx_len % CHUNK == 0
    assert batch + 1 <= 8

    x_l = x.reshape(m_lat, d)
    x_c = ctx.reshape(m_ctx, d)
    cond = jnp.zeros((8, d), F32).at[:batch].set(c).at[batch].set(c_ctx)
    a0 = 2 * d_key + 2 * d
    a1 = a0 + 2 * GATE_RANK
    w_tail = w_in[:, :, a1:].astype(BF16)
    w_alr = jnp.pad(w_in[:, :, a0:a1], ((0, 0), (0, 0), (0, LANES - 2 * GATE_RANK))).astype(BF16)
    wup = jnp.zeros((depth, 2, LANES, d_key), F32)
    wup = wup.at[:, 0, :GATE_RANK].set(w_decay_up[:, 0]).at[:, 1, GATE_RANK:2 * GATE_RANK].set(w_decay_up[:, 1])
    wup = wup.astype(BF16)
    bup = b_decay_up.reshape(depth, 2, 1, d_key)
    gain = gla_norm_gain.reshape(depth, 1, d)
    wpg = w_pool_group.astype(BF16)
    pscale = pool_scale.reshape(depth, 1, d_pool)
    w_out_b = w_out.astype(BF16)
    w_ffn_out_b = w_ffn_out.astype(BF16)
    lnm_g = ln_mix_gain.reshape(depth, 1, d)
    lnm_b = ln_mix_bias.reshape(depth, 1, d)
    lnf_g = ln_ffn_gain.reshape(depth, 1, d)
    lnf_b = ln_ffn_bias.reshape(depth, 1, d)
    n_off = max(len(_box_offsets(w)) for w in POOL_WINDOWS)
    assert max(abs(o) for w in POOL_WINDOWS for o in _box_offsets(w)) <= POOL_HALO_TILES
    box_np = np.zeros((len(POOL_WINDOWS), n_off, POOL_TILE, POOL_TILE), np.float32)
    for gi, w in enumerate(POOL_WINDOWS):
        for j, off in enumerate(_box_offsets(w)):
            box_np[gi, j] = _box_matrix(w, off)
    box_lat = jnp.asarray(box_np, BF16)
    band_ctx = jnp.asarray(np.stack([_band_matrix(ctx_len, w) for w in POOL_WINDOWS]), BF16)
    t = np.arange(ctx_len)
    icnt_ctx = jnp.asarray(np.stack([
        1.0 / (np.minimum(t + _win(w)[1] + 1, ctx_len) - np.maximum(t - _win(w)[0], 0))
        for w in POOL_WINDOWS]).astype(np.float32)[:, :, None])

    mods = _mods_call(cond, w_ada, b_ada)

    lat_row_1024 = lambda i: i // (seq // LAT_TM)
    lat_row_512 = lambda i: i // (seq // LN_TM)
    ctx_row = lambda i: batch

    h_l = _modulate_call(x_l, mods, 0, lat_row_1024, LAT_TM)
    h_c = _modulate_call(x_c, mods, 0, ctx_row, m_ctx)

    ffn_tk = d_ff // 4
    for layer in range(depth):
        ctx_out = layer < depth - 1
        proj_l = _in_proj_call(h_l, w_in, w_tail, gain, layer, LAT_TM, 1024, "in_proj")
        alr_l = _mm_call(h_l, w_alr, layer, LAT_TM, LANES, F32, "in_proj_decay")
        proj_c = _in_proj_call(h_c, w_in, w_tail, gain, layer, m_ctx, 1024, "in_proj_ctx")
        alr_c = _mm_call(h_c, w_alr, layer, m_ctx, LANES, F32, "in_proj_decay_ctx")
        gla_l, gla_c = _gla_call(proj_l, alr_l, proj_c, alr_c, wup, bup, layer,
                                 batch, seq, ctx_len, d)
        pool_l = _pool_lat_call(proj_l, box_lat, wpg, pscale, layer, batch, seq, d)
        merged_l = _merge_call(gla_l, pool_l, proj_l, w_gla_out, w_pool_out, layer, LAT_TM, 512)
        x_l, hf_l = _mm_ln_call(merged_l, w_out_b, x_l, mods, lnm_g, lnm_b, layer, 2,
                                (layer, 4, 3), lat_row_512, LN_TM, d, alpha)
        act_l = _ffn1_call(hf_l, w_ffn_in, layer, LAT_TM, 512)
        nxt = (layer + 1, 1, 0) if ctx_out else None
        x_l, h_l = _mm_ln_call(act_l, w_ffn_out_b, x_l, mods, lnf_g, lnf_b, layer, 5,
                               nxt, lat_row_512, LN_TM, ffn_tk, alpha)
        if ctx_out:
            pool_c = _pool_ctx_call(proj_c, band_ctx, icnt_ctx, wpg, pscale, layer, batch, ctx_len, d)
            merged_c = _merge_call(gla_c, pool_c, proj_c, w_gla_out, w_pool_out, layer, m_ctx, 512)
            x_c, hf_c = _mm_ln_call(merged_c, w_out_b, x_c, mods, lnm_g, lnm_b, layer, 2,
                                    (layer, 4, 3), ctx_row, m_ctx, d, alpha)
            act_c = _ffn1_call(hf_c, w_ffn_in, layer, m_ctx, 512)
            x_c, h_c = _mm_ln_call(act_c, w_ffn_out_b, x_c, mods, lnf_g, lnf_b, layer, 5,
                                   (layer + 1, 1, 0), ctx_row, m_ctx, ffn_tk, alpha)
    return x_l.reshape(batch, seq, d)
```

```python
import functools

import numpy as np
import jax
import jax.numpy as jnp
from jax import lax
from jax.experimental import pallas as pl
from jax.experimental.pallas import tpu as pltpu

F32 = jnp.float32
BF16 = jnp.bfloat16

GRID_W = 64
N_HEADS = 4
GATE_RANK = 16
GATE_NORM = 16.0
CHUNK = 64
POOL_WINDOWS = (2, 4, 8, 16)
N_MOD = 6
LN_EPS = 1e-5
RMS_EPS = 1e-6
LOG2E = 1.4426950408889634

LANES = 128
VMEM_LIMIT = 56 * 1024 * 1024

GLA_BLOCK = 512
GLA_FINAL_ROWS = 2048
POOL_TILE = 256
POOL_HALO_TILES = 2
LAT_TM = 1024
LN_TM = 512
LN_ROWS = 16

def _cparams(n_axes, vmem=VMEM_LIMIT):
    return pltpu.CompilerParams(dimension_semantics=("arbitrary",) * n_axes,
                                vmem_limit_bytes=vmem)


def _silu(v):
    return v * jax.nn.sigmoid(v)


def _mods_kernel(cond_ref, w_ref, b_ref, o_ref):
    a = _silu(cond_ref[...]).astype(BF16)
    o_ref[...] = jnp.dot(a, w_ref[...].astype(BF16), preferred_element_type=F32) + b_ref[...]


def _mods_call(cond, w_ada, b_ada):
    depth, d, n = w_ada.shape
    tn = 1024
    return pl.pallas_call(
        _mods_kernel,
        out_shape=jax.ShapeDtypeStruct((depth, 8, n), F32),
        grid=(depth, n // tn),
        in_specs=[pl.BlockSpec((8, d), lambda l, j: (0, 0)),
                  pl.BlockSpec((None, d, tn), lambda l, j: (l, 0, j)),
                  pl.BlockSpec((None, 1, tn), lambda l, j: (l, 0, j))],
        out_specs=pl.BlockSpec((None, 8, tn), lambda l, j: (l, 0, j)),
        compiler_params=_cparams(2),
        name="mods",
    )(cond, w_ada, b_ada.reshape(depth, 1, n))


def _mod_spec(layer, chunk, d):
    return pl.BlockSpec((None, 8, d), lambda *_: (layer, 0, chunk))


def _modulate_kernel(x_ref, sc_ref, sh_ref, o_ref, *, row_fn):
    r = row_fn(pl.program_id(0))
    sc = sc_ref[pl.ds(r, 1), :]
    sh = sh_ref[pl.ds(r, 1), :]
    o_ref[...] = (x_ref[...] * (1.0 + sc) + sh).astype(BF16)


def _modulate_call(x, mods, layer, row_fn, tm):
    m, d = x.shape
    return pl.pallas_call(
        functools.partial(_modulate_kernel, row_fn=row_fn),
        out_shape=jax.ShapeDtypeStruct((m, d), BF16),
        grid=(m // tm,),
        in_specs=[pl.BlockSpec((tm, d), lambda i: (i, 0)),
                  _mod_spec(layer, 1, d), _mod_spec(layer, 0, d)],
        out_specs=pl.BlockSpec((tm, d), lambda i: (i, 0)),
        compiler_params=_cparams(1),
        name="modulate",
    )(x, mods, mods)


def _in_proj_kernel(a_ref, wt_ref, gain_ref, o_ref, wb_ref, *, n_q, g_lo, g_hi, q_scale):
    j = pl.program_id(0)
    is_q = j < n_q
    is_g = jnp.logical_and(j >= g_lo, j < g_hi)

    is_plain = jnp.logical_not(jnp.logical_or(is_q, is_g))
    first = pl.program_id(1) == 0

    for cast in (True, False):
        def mm(cast=cast):
            if cast:
                wb = wt_ref[0].astype(BF16)
                wb_ref[...] = wb
            else:
                wb = wb_ref[...]
            return lax.dot_general(a_ref[...], wb, (((1,), (1,)), ((), ())),
                                   preferred_element_type=F32)

        step = first if cast else jnp.logical_not(first)

        @pl.when(jnp.logical_and(step, is_q))
        def _(mm=mm):
            o_ref[...] = (mm() * q_scale).astype(o_ref.dtype)

        @pl.when(jnp.logical_and(step, is_g))
        def _(mm=mm):
            o_ref[...] = (_silu(mm()) * gain_ref[...]).astype(o_ref.dtype)

        @pl.when(jnp.logical_and(step, is_plain))
        def _(mm=mm):
            o_ref[...] = mm().astype(o_ref.dtype)


def _in_proj_call(a, w_in_t, gain, layer, tm, tn, name):
    m, k = a.shape
    d = gain.shape[-1]
    d_key = d // 2
    n_head = 2 * d_key + 2 * d
    n = w_in_t.shape[1] - 2 * GATE_RANK
    n_a = n_head // tn
    g_lo = (2 * d_key + d) // tn
    g_hi = g_lo + d // tn
    kern = functools.partial(_in_proj_kernel, n_q=d_key // tn, g_lo=g_lo, g_hi=g_hi,
                             q_scale=(d_key // N_HEADS) ** -0.5)

    def row_start(j):
        sub = 8
        return (j * (tn // sub) + jnp.where(j >= n_a, 2 * GATE_RANK // sub, 0)) * sub

    return pl.pallas_call(
        kern,
        out_shape=jax.ShapeDtypeStruct((m, n), BF16),
        grid=(n // tn, m // tm),
        in_specs=[pl.BlockSpec((tm, k), lambda j, i: (i, 0)),
                  pl.BlockSpec((pl.Element(1), pl.Element(tn), pl.Element(k)),
                               lambda j, i: (layer, row_start(j), 0)),
                  pl.BlockSpec((None, 1, tn),
                               lambda j, i: (layer, 0, jnp.clip(j - g_lo, 0, g_hi - g_lo - 1)))],
        out_specs=pl.BlockSpec((tm, tn), lambda j, i: (i, j)),
        scratch_shapes=[pltpu.VMEM((tn, k), BF16)],
        compiler_params=_cparams(2),
        name=name,
    )(a, w_in_t, gain)


def _in_proj_decay_kernel(a_ref, wt_ref, o_ref, *, n_valid):
    y = lax.dot_general(a_ref[...], wt_ref[...].astype(BF16), (((1,), (1,)), ((), ())),
                        preferred_element_type=F32)
    lane = lax.broadcasted_iota(jnp.int32, y.shape, 1)
    o_ref[...] = jnp.where(lane < n_valid, y, 0.0)


def _in_proj_decay_call(a, w_in_t, layer, tm, d, name):
    m, k = a.shape
    row0 = 2 * (d // 2) + 2 * d
    assert row0 % LANES == 0
    return pl.pallas_call(
        functools.partial(_in_proj_decay_kernel, n_valid=2 * GATE_RANK),
        out_shape=jax.ShapeDtypeStruct((m, LANES), F32),
        grid=(m // tm,),
        in_specs=[pl.BlockSpec((tm, k), lambda i: (i, 0)),
                  pl.BlockSpec((None, LANES, k), lambda i: (layer, row0 // LANES, 0))],
        out_specs=pl.BlockSpec((tm, LANES), lambda i: (i, 0)),
        compiler_params=_cparams(1),
        name=name,
    )(a, w_in_t)


def _gla_kernel(qf_ref, kf_ref, vf_ref, af_ref, qb_ref, kb_ref, vb_ref, ab_ref, g_ref,
                qc_ref, kc_ref, vc_ref, ac_ref, gc_ref,
                wup_ref, bup_ref,
                out_ref, outc_ref,
                state_ref, acc_ref, accc_ref, *, nb):
    s = pl.program_id(2)
    fin_rows = out_ref.shape[0]

    row = lax.broadcasted_iota(jnp.int32, (CHUNK, CHUNK), 0)
    col = lax.broadcasted_iota(jnp.int32, (CHUNK, CHUNK), 1)
    keep = (row >= col, row <= col)
    ref_row = (CHUNK // 2 - 1, CHUNK // 2)
    last_row = (CHUNK - 1, 0)

    def log_decay(d, ar):
        z = jnp.dot(ar[...].astype(BF16), wup_ref[d], preferred_element_type=F32) + bup_ref[d]
        log_a = ((jnp.minimum(z, 0.0) - jnp.log(1.0 + jnp.exp2(jnp.abs(z) * (-LOG2E))))
                 * (LOG2E / GATE_NORM))
        hi = log_a.astype(BF16)
        return hi, (log_a - hi.astype(F32)).astype(BF16)

    def intra(d, tri, hi, lo, qr, kr, vr, c):
        sl = slice(c * CHUNK, (c + 1) * CHUNK)
        cum = (jnp.dot(tri, hi[sl], preferred_element_type=F32)
               + jnp.dot(tri, lo[sl], preferred_element_type=F32))
        ref = cum[ref_row[d]:ref_row[d] + 1, :]
        last = cum[last_row[d]:last_row[d] + 1, :]
        q = qr[sl, :].astype(F32)
        k = kr[sl, :].astype(F32)
        v = vr[sl, :]
        q_in = (q * jnp.exp2(cum - ref)).astype(BF16)
        k_in = (k * jnp.exp2(ref - cum)).astype(BF16)
        scores = lax.dot_general(q_in, k_in, (((1,), (1,)), ((), ())), preferred_element_type=F32)
        scores = jnp.where(keep[d], scores, 0.0).astype(BF16)
        o_intra = jnp.dot(scores, v, preferred_element_type=F32)
        q_inter = (q * jnp.exp2(cum)).astype(BF16)
        k_state = (k * jnp.exp2(last - cum)).astype(BF16)
        upd = lax.dot_general(v, k_state, (((0,), (0,)), ((), ())), preferred_element_type=F32)
        return o_intra, q_inter, upd, jnp.exp2(last)

    def scan_pair(fwd, bwd, acc, off_f, off_b):
        streams = []
        for d, (qr, kr, vr, ar), off in ((0, fwd, off_f), (1, bwd, off_b)):
            n_chunks = qr.shape[0] // CHUNK
            order = list(range(n_chunks - 1, -1, -1) if d == 1 else range(n_chunks))
            hi, lo = log_decay(d, ar)
            tri = jnp.where(keep[d], 1.0, 0.0).astype(BF16)
            streams.append((d, tri, qr, kr, vr, hi, lo, order, off))
        n_steps = len(streams[0][7])
        pre = {}
        for i in range(n_steps):
            for d, tri, qr, kr, vr, hi, lo, order, _ in streams:
                pre[d, i] = intra(d, tri, hi, lo, qr, kr, vr, order[i])
        st = [state_ref[0], state_ref[1]]
        for i in range(n_steps):
            for d, _, _, _, _, _, _, order, off in streams:
                o_intra, q_inter, upd, decay = pre[d, i]
                o = o_intra + lax.dot_general(q_inter, st[d].astype(BF16), (((1,), (1,)), ((), ())),
                                              preferred_element_type=F32)
                st[d] = st[d] * decay + upd
                ra = pl.multiple_of(off + order[i] * CHUNK, CHUNK)
                acc[pl.ds(ra, CHUNK), :] += o
        state_ref[0] = st[0]
        state_ref[1] = st[1]

    def finalize(acc, base, gr, outr):
        def body(c, carry):
            r0 = pl.multiple_of(c * CHUNK, CHUNK)
            o = acc[pl.ds(pl.multiple_of(base + r0, CHUNK), CHUNK), :]
            ms = jnp.mean(o * o, axis=-1, keepdims=True)
            y = o * lax.rsqrt(ms + RMS_EPS) * gr[pl.ds(r0, CHUNK), :].astype(F32)
            outr[pl.ds(r0, CHUNK), :] = y.astype(outr.dtype)
            return carry

        lax.fori_loop(0, outr.shape[0] // CHUNK, body, 0)

    ctx = (qc_ref, kc_ref, vc_ref, ac_ref)

    @pl.when(s == 0)
    def _():
        state_ref[...] = jnp.zeros_like(state_ref)
        acc_ref[...] = jnp.zeros_like(acc_ref)
        accc_ref[...] = jnp.zeros_like(accc_ref)
        scan_pair(ctx, ctx, accc_ref, 0, 0)

    @pl.when(s < nb)
    def _():
        scan_pair((qf_ref, kf_ref, vf_ref, af_ref), (qb_ref, kb_ref, vb_ref, ab_ref), acc_ref,
                  s * GLA_BLOCK, (nb - 1 - s) * GLA_BLOCK)

    @pl.when(s == nb)
    def _():
        finalize(accc_ref, 0, gc_ref, outc_ref)

    @pl.when(s >= nb)
    def _():
        finalize(acc_ref, (s - nb) * fin_rows, g_ref, out_ref)


def _gla_call(proj_l, alr_l, proj_c, alr_c, wup, bup, layer, batch, seq, ctx_len, d_model):
    d_key = d_model // 2
    head_k = d_key // N_HEADS
    head_v = d_model // N_HEADS
    nb = seq // GLA_BLOCK
    fin_rows = min(GLA_FINAL_ROWS, seq)
    nf = seq // fin_rows
    kq = d_key // head_k
    kv = (2 * d_key) // head_v
    kg = kv + d_model // head_v

    def blk(s, rev):
        return jnp.maximum(nb - 1 - s, 0) if rev else jnp.minimum(s, nb - 1)

    def lat(width, col0, rev):
        return pl.BlockSpec((GLA_BLOCK, width), lambda b, h, s: (b * nb + blk(s, rev), col0 + h))

    def lat_a(rev):
        return pl.BlockSpec((GLA_BLOCK, LANES), lambda b, h, s: (b * nb + blk(s, rev), 0))

    def fin(col0):
        return pl.BlockSpec((fin_rows, head_v),
                            lambda b, h, s: (b * nf + jnp.clip(s - nb, 0, nf - 1), col0 + h))

    def cx(width, col0):
        return pl.BlockSpec((ctx_len, width), lambda b, h, s: (b, col0 + h))

    in_specs = []
    for rev in (False, True):
        in_specs += [lat(head_k, 0, rev), lat(head_k, kq, rev), lat(head_v, kv, rev), lat_a(rev)]
    in_specs += [fin(kg),
                 cx(head_k, 0), cx(head_k, kq), cx(head_v, kv),
                 pl.BlockSpec((ctx_len, LANES), lambda b, h, s: (b, 0)),
                 cx(head_v, kg),
                 pl.BlockSpec((None, 2, LANES, head_k), lambda b, h, s: (layer, 0, 0, h)),
                 pl.BlockSpec((None, 2, 1, head_k), lambda b, h, s: (layer, 0, 0, h))]
    out_specs = [fin(0), cx(head_v, 0)]
    return pl.pallas_call(
        functools.partial(_gla_kernel, nb=nb),
        out_shape=(jax.ShapeDtypeStruct((batch * seq, d_model), BF16),
                   jax.ShapeDtypeStruct((batch * ctx_len, d_model), BF16)),
        grid=(batch, N_HEADS, nb + nf),
        in_specs=in_specs,
        out_specs=out_specs,
        scratch_shapes=[pltpu.VMEM((2, head_v, head_k), F32),
                        pltpu.VMEM((seq, head_v), F32),
                        pltpu.VMEM((ctx_len, head_v), F32)],
        compiler_params=_cparams(3),
        name="gla",
    )(proj_l, proj_l, proj_l, alr_l, proj_l, proj_l, proj_l, alr_l, proj_l,
      proj_c, proj_c, proj_c, alr_c, proj_c, wup, bup)


def _win(w):
    lo = w // 2
    return lo, w - lo - 1


def _band_matrix(n_tokens, w):
    lo, hi = _win(w)
    t = np.arange(n_tokens)
    off = t[None, :] - t[:, None]
    return ((off >= -lo) & (off <= hi)).astype(np.float32)


def _box_offsets(w):
    lo, hi = _win(w)
    rpt = POOL_TILE // GRID_W
    return list(range(-((lo + rpt - 1) // rpt), (hi + rpt - 1) // rpt + 1))


def _box_matrix(w, tile_offset):
    lo, hi = _win(w)
    rpt = POOL_TILE // GRID_W
    t = np.arange(POOL_TILE)
    a, c = t // GRID_W, t % GRID_W
    drow = rpt * tile_offset + a[None, :] - a[:, None]
    dcol = c[None, :] - c[:, None]
    return ((drow >= -lo) & (drow <= hi) & (dcol >= -lo) & (dcol <= hi)).astype(np.float32)


def _count(idx, n, w):
    lo, hi = _win(w)
    return jnp.minimum(idx + hi + 1, n) - jnp.maximum(idx - lo, 0)


def _pool_lat_kernel(p_ref, box_ref, wg_ref, sc_ref, o_ref, ps_ref, *, n_rows):
    g = pl.program_id(1)
    seq = n_rows * GRID_W
    halo = POOL_HALO_TILES * POOL_TILE
    n_tiles = seq // POOL_TILE
    zeros = jnp.zeros((halo, ps_ref.shape[1]), ps_ref.dtype)
    ps_ref[pl.ds(0, halo), :] = zeros
    ps_ref[pl.ds(halo + seq, halo), :] = zeros
    ps_ref[pl.ds(halo, seq), :] = p_ref[...]

    for gi, w in enumerate(POOL_WINDOWS):
        offsets = _box_offsets(w)

        @pl.when(g == gi)
        def _(w=w, offsets=offsets):
            def tile(t, carry):
                r0 = pl.multiple_of(t * POOL_TILE, POOL_TILE)
                tot = None
                for j, off in enumerate(offsets):
                    src = ps_ref[pl.ds(halo + r0 + off * POOL_TILE, POOL_TILE), :]
                    part = jnp.dot(box_ref[j], src, preferred_element_type=F32)
                    tot = part if tot is None else tot + part
                tok = r0 + lax.broadcasted_iota(jnp.int32, (POOL_TILE, 1), 0)
                img_row = lax.shift_right_logical(tok, GRID_W.bit_length() - 1)
                img_col = jnp.bitwise_and(tok, GRID_W - 1)
                cnt = _count(img_row, n_rows, w) * _count(img_col, GRID_W, w)
                pin = p_ref[pl.ds(r0, POOL_TILE), :].astype(F32)
                diff = (tot * (1.0 / cnt.astype(F32)) - pin).astype(BF16)
                y = jnp.dot(diff, wg_ref[...], preferred_element_type=F32) * sc_ref[...]
                o_ref[pl.ds(r0, POOL_TILE), :] = y.astype(o_ref.dtype)
                return carry

            lax.fori_loop(0, n_tiles, tile, 0, unroll=4)


def _pool_lat_call(proj_l, box, wpg, pscale, layer, batch, seq, d_model):
    d_pool = d_model // 2
    grp = d_pool // len(POOL_WINDOWS)
    col0 = (d_model // 2 * 2 + 2 * d_model) // grp
    halo = POOL_HALO_TILES * POOL_TILE
    return pl.pallas_call(
        functools.partial(_pool_lat_kernel, n_rows=seq // GRID_W),
        out_shape=jax.ShapeDtypeStruct((batch * seq, d_pool), BF16),
        grid=(batch, len(POOL_WINDOWS)),
        in_specs=[pl.BlockSpec((seq, grp), lambda b, g: (b, col0 + g)),
                  pl.BlockSpec((None, box.shape[1], POOL_TILE, POOL_TILE), lambda b, g: (g, 0, 0, 0)),
                  pl.BlockSpec((None, None, grp, grp), lambda b, g: (layer, g, 0, 0)),
                  pl.BlockSpec((None, 1, grp), lambda b, g: (layer, 0, g))],
        out_specs=pl.BlockSpec((seq, grp), lambda b, g: (b, g)),
        scratch_shapes=[pltpu.VMEM((seq + 2 * halo, grp), BF16)],
        compiler_params=_cparams(2),
        name="pool_lat",
    )(proj_l, box, wpg, pscale)


def _pool_ctx_kernel(p_ref, band_ref, icnt_ref, wg_ref, sc_ref, o_ref):
    tot = jnp.dot(band_ref[...], p_ref[...], preferred_element_type=F32)
    diff = (tot * icnt_ref[...] - p_ref[...].astype(F32)).astype(BF16)
    y = jnp.dot(diff, wg_ref[...], preferred_element_type=F32) * sc_ref[...]
    o_ref[...] = y.astype(o_ref.dtype)


def _pool_ctx_call(proj_c, band, icnt, wpg, pscale, layer, batch, ctx_len, d_model):
    d_pool = d_model // 2
    grp = d_pool // len(POOL_WINDOWS)
    col0 = (d_model // 2 * 2 + 2 * d_model) // grp
    return pl.pallas_call(
        _pool_ctx_kernel,
        out_shape=jax.ShapeDtypeStruct((batch * ctx_len, d_pool), BF16),
        grid=(batch, len(POOL_WINDOWS)),
        in_specs=[pl.BlockSpec((ctx_len, grp), lambda b, g: (b, col0 + g)),
                  pl.BlockSpec((None, ctx_len, ctx_len), lambda b, g: (g, 0, 0)),
                  pl.BlockSpec((None, ctx_len, 1), lambda b, g: (g, 0, 0)),
                  pl.BlockSpec((None, None, grp, grp), lambda b, g: (layer, g, 0, 0)),
                  pl.BlockSpec((None, 1, grp), lambda b, g: (layer, 0, g))],
        out_specs=pl.BlockSpec((ctx_len, grp), lambda b, g: (b, g)),
        compiler_params=_cparams(2),
        name="pool_ctx",
    )(proj_c, band, icnt, wpg, pscale)


def _merge_kernel(ga_ref, pa_ref, bgg_ref, bgp_ref, wg_ref, wp_ref, o_ref, wgb_ref, wpb_ref):
    first = pl.program_id(1) == 0

    def merge(wg, wp):
        yg = jnp.dot(ga_ref[...], wg, preferred_element_type=F32)
        yp = jnp.dot(pa_ref[...], wp, preferred_element_type=F32)
        y = (jax.nn.sigmoid(bgg_ref[...].astype(F32)) * yg
             + jax.nn.sigmoid(bgp_ref[...].astype(F32)) * yp)
        o_ref[...] = y.astype(o_ref.dtype)

    @pl.when(first)
    def _():
        wg = wg_ref[...].astype(BF16)
        wp = wp_ref[...].astype(BF16)
        wgb_ref[...] = wg
        wpb_ref[...] = wp
        merge(wg, wp)

    @pl.when(jnp.logical_not(first))
    def _():
        merge(wgb_ref[...], wpb_ref[...])


def _merge_call(gla_act, pool_act, proj, w_gla_out, w_pool_out, layer, tm, tn, n_rows=None):
    m = gla_act.shape[0] if n_rows is None else n_rows
    d = w_gla_out.shape[-1]
    d_pool = pool_act.shape[1]
    bg0 = (proj.shape[1] - 2 * d) // tn
    return pl.pallas_call(
        _merge_kernel,
        out_shape=jax.ShapeDtypeStruct((m, d), BF16),
        grid=(d // tn, m // tm),
        in_specs=[pl.BlockSpec((tm, d), lambda j, i: (i, 0)),
                  pl.BlockSpec((tm, d_pool), lambda j, i: (i, 0)),
                  pl.BlockSpec((tm, tn), lambda j, i: (i, bg0 + j)),
                  pl.BlockSpec((tm, tn), lambda j, i: (i, bg0 + d // tn + j)),
                  pl.BlockSpec((None, d, tn), lambda j, i: (layer, 0, j)),
                  pl.BlockSpec((None, d_pool, tn), lambda j, i: (layer, 0, j))],
        out_specs=pl.BlockSpec((tm, tn), lambda j, i: (i, j)),
        scratch_shapes=[pltpu.VMEM((d, tn), BF16), pltpu.VMEM((d_pool, tn), BF16)],
        compiler_params=_cparams(2),
        name="merge",
    )(gla_act, pool_act, proj, proj, w_gla_out, w_pool_out)


def _mm_ln_kernel(*refs, row_fn, nm, nk, alpha, emit_h):
    if emit_h:
        (a_ref, w_ref, x_ref, gt_ref, lng_ref, lnb_ref, sc_ref, sh_ref, xo_ref, ho_ref, *acc) = refs
    else:
        (a_ref, w_ref, x_ref, gt_ref, lng_ref, lnb_ref, xo_ref, *acc) = refs
    i = pl.program_id(0)
    kk = pl.program_id(1)
    tq = x_ref.shape[0] // nk

    def matmul(acc_ref):
        part = jnp.dot(a_ref[...], w_ref[...], preferred_element_type=F32)
        if nk == 1:
            acc_ref[...] = part
        else:
            acc_ref[...] += part

    def epilogue(acc_ref):
        base = kk * tq
        r = row_fn(i - 1)
        gt = gt_ref[pl.ds(r, 1), :]
        lng = lng_ref[...]
        lnb = lnb_ref[...]
        if emit_h:
            sc1 = 1.0 + sc_ref[pl.ds(r, 1), :]
            sh = sh_ref[pl.ds(r, 1), :]
        for c in range(tq // LN_ROWS):
            rows = pl.ds(pl.multiple_of(base + c * LN_ROWS, LN_ROWS), LN_ROWS)
            y = alpha * x_ref[rows, :] + gt * acc_ref[rows, :]
            mu = jnp.mean(y, axis=-1, keepdims=True)
            yc = y - mu
            var = jnp.mean(yc * yc, axis=-1, keepdims=True)
            xn = yc * lax.rsqrt(var + LN_EPS) * lng + lnb
            xo_ref[rows, :] = xn
            if emit_h:
                ho_ref[rows, :] = (xn * sc1 + sh).astype(BF16)
            if nk > 1:
                acc_ref[rows, :] = jnp.zeros((LN_ROWS, acc_ref.shape[1]), F32)

    @pl.when(i == 0)
    def _():
        if nk > 1:
            @pl.when(kk == 0)
            def _():
                acc[0][...] = jnp.zeros_like(acc[0])
                acc[1][...] = jnp.zeros_like(acc[1])
        matmul(acc[0])

    for parity in (0, 1):
        @pl.when(jnp.logical_and(jnp.logical_and(i >= 1, i < nm), i % 2 == parity))
        def _(parity=parity):
            epilogue(acc[1 - parity])
            matmul(acc[parity])

    @pl.when(i == nm)
    def _():
        epilogue(acc[(nm - 1) % 2])


def _mm_ln_call(a, w, x, mods, ln_gain, ln_bias, layer, gate_chunk, next_mod, row_fn, tm, tk, alpha,
                n_rows=None):
    m = x.shape[0] if n_rows is None else n_rows
    k = a.shape[1]
    d = x.shape[1]
    nk = k // tk
    nm = m // tm
    emit_h = next_mod is not None

    def k_idx(i, kk):
        return jnp.where(i < nm, kk, nk - 1)

    prev = lambda i, kk: (jnp.maximum(i - 1, 0), 0)
    in_specs = [pl.BlockSpec((tm, tk), lambda i, kk: (jnp.minimum(i, nm - 1), k_idx(i, kk))),
                pl.BlockSpec((None, tk, d), lambda i, kk: (layer, k_idx(i, kk), 0)),
                pl.BlockSpec((tm, d), prev),
                _mod_spec(layer, gate_chunk, d),
                pl.BlockSpec((None, 1, d), lambda i, kk: (layer, 0, 0)),
                pl.BlockSpec((None, 1, d), lambda i, kk: (layer, 0, 0))]
    args = [a, w, x, mods, ln_gain, ln_bias]
    out_shape = [jax.ShapeDtypeStruct((m, d), F32)]
    out_specs = [pl.BlockSpec((tm, d), prev)]
    if emit_h:
        nl, sc_chunk, sh_chunk = next_mod
        in_specs += [_mod_spec(nl, sc_chunk, d), _mod_spec(nl, sh_chunk, d)]
        args += [mods, mods]
        out_shape.append(jax.ShapeDtypeStruct((m, d), BF16))
        out_specs.append(pl.BlockSpec((tm, d), prev))
    outs = pl.pallas_call(
        functools.partial(_mm_ln_kernel, row_fn=row_fn, nm=nm, nk=nk, alpha=alpha, emit_h=emit_h),
        out_shape=out_shape,
        grid=(nm + 1, nk),
        in_specs=in_specs,
        out_specs=out_specs,
        scratch_shapes=[pltpu.VMEM((tm, d), F32), pltpu.VMEM((tm, d), F32)],
        compiler_params=_cparams(2),
        name="mm_ln",
    )(*args)
    return (outs[0], outs[1]) if emit_h else (outs[0], None)


def _ffn1_kernel(h_ref, wg_ref, wu_ref, o_ref, wgb_ref, wub_ref):
    first = pl.program_id(1) == 0

    def act(wg, wu):
        h = h_ref[...]
        gate = jnp.dot(h, wg, preferred_element_type=F32)
        up = jnp.dot(h, wu, preferred_element_type=F32)
        o_ref[...] = (_silu(gate) * up).astype(o_ref.dtype)

    @pl.when(first)
    def _():
        wg = wg_ref[...].astype(BF16)
        wu = wu_ref[...].astype(BF16)
        wgb_ref[...] = wg
        wub_ref[...] = wu
        act(wg, wu)

    @pl.when(jnp.logical_not(first))
    def _():
        act(wgb_ref[...], wub_ref[...])


def _ffn1_call(h, w_ffn_in, layer, tm, tn, n_rows=None):
    m = h.shape[0] if n_rows is None else n_rows
    d = h.shape[1]
    d_ff = w_ffn_in.shape[-1] // 2
    nj = d_ff // tn
    return pl.pallas_call(
        _ffn1_kernel,
        out_shape=jax.ShapeDtypeStruct((m, d_ff), BF16),
        grid=(nj, m // tm),
        in_specs=[pl.BlockSpec((tm, d), lambda j, i: (i, 0)),
                  pl.BlockSpec((None, d, tn), lambda j, i: (layer, 0, j)),
                  pl.BlockSpec((None, d, tn), lambda j, i: (layer, 0, nj + j))],
        out_specs=pl.BlockSpec((tm, tn), lambda j, i: (i, j)),
        scratch_shapes=[pltpu.VMEM((d, tn), BF16), pltpu.VMEM((d, tn), BF16)],
        compiler_params=_cparams(2),
        name="ffn1",
    )(h, w_ffn_in, w_ffn_in)


def kernel(x, c, ctx, c_ctx, w_ada, b_ada, w_in, w_decay_up, b_decay_up, gla_norm_gain, w_pool_group, pool_scale, w_gla_out, w_pool_out, w_out, ln_mix_gain, ln_mix_bias, w_ffn_in, w_ffn_out, ln_ffn_gain, ln_ffn_bias):
    batch, seq, d = x.shape
    ctx_len = ctx.shape[1]
    depth = w_ada.shape[0]
    d_key = d // 2
    d_pool = d // 2
    d_ff = w_ffn_out.shape[1]
    alpha = (2.0 * depth) ** 0.25
    m_lat = batch * seq
    m_ctx = batch * ctx_len
    assert seq % (2 * GLA_BLOCK) == 0 and seq % LAT_TM == 0 and ctx_len % CHUNK == 0
    assert batch + 1 <= 8

    x_l = x.reshape(m_lat, d)
    x_c = ctx.reshape(m_ctx, d)
    cond = jnp.zeros((8, d), F32).at[:batch].set(c).at[batch].set(c_ctx)
    w_in_t = jnp.swapaxes(w_in, 1, 2)
    wup = jnp.zeros((depth, 2, LANES, d_key), F32)
    wup = wup.at[:, 0, :GATE_RANK].set(w_decay_up[:, 0]).at[:, 1, GATE_RANK:2 * GATE_RANK].set(w_decay_up[:, 1])
    wup = wup.astype(BF16)
    bup = b_decay_up.reshape(depth, 2, 1, d_key)
    gain = gla_norm_gain.reshape(depth, 1, d)
    wpg = w_pool_group.astype(BF16)
    pscale = pool_scale.reshape(depth, 1, d_pool)
    w_out_b = w_out.astype(BF16)
    w_ffn_out_b = w_ffn_out.astype(BF16)
    lnm_g = ln_mix_gain.reshape(depth, 1, d)
    lnm_b = ln_mix_bias.reshape(depth, 1, d)
    lnf_g = ln_ffn_gain.reshape(depth, 1, d)
    lnf_b = ln_ffn_bias.reshape(depth, 1, d)
    n_off = max(len(_box_offsets(w)) for w in POOL_WINDOWS)
    assert max(abs(o) for w in POOL_WINDOWS for o in _box_offsets(w)) <= POOL_HALO_TILES
    box_np = np.zeros((len(POOL_WINDOWS), n_off, POOL_TILE, POOL_TILE), np.float32)
    for gi, w in enumerate(POOL_WINDOWS):
        for j, off in enumerate(_box_offsets(w)):
            box_np[gi, j] = _box_matrix(w, off)
    box_lat = jnp.asarray(box_np, BF16)
    band_ctx = jnp.asarray(np.stack([_band_matrix(ctx_len, w) for w in POOL_WINDOWS]), BF16)
    t = np.arange(ctx_len)
    icnt_ctx = jnp.asarray(np.stack([
        1.0 / (np.minimum(t + _win(w)[1] + 1, ctx_len) - np.maximum(t - _win(w)[0], 0))
        for w in POOL_WINDOWS]).astype(np.float32)[:, :, None])

    mods = _mods_call(cond, w_ada, b_ada)

    lat_row_1024 = lambda i: i // (seq // LAT_TM)
    lat_row_512 = lambda i: i // (seq // LN_TM)
    ctx_row = lambda i: batch

    h_l = _modulate_call(x_l, mods, 0, lat_row_1024, LAT_TM)
    h_c = _modulate_call(x_c, mods, 0, ctx_row, m_ctx)

    ffn_tk = d_ff // 4
    for layer in range(depth):
        ctx_out = layer < depth - 1
        proj_l = _in_proj_call(h_l, w_in_t, gain, layer, LAT_TM, 1024, "in_proj")
        alr_l = _in_proj_decay_call(h_l, w_in_t, layer, LAT_TM, d, "in_proj_decay")
        proj_c = _in_proj_call(h_c, w_in_t, gain, layer, m_ctx, 1024, "in_proj_ctx")
        alr_c = _in_proj_decay_call(h_c, w_in_t, layer, m_ctx, d, "in_proj_decay_ctx")
        gla_l, gla_c = _gla_call(proj_l, alr_l, proj_c, alr_c, wup, bup, layer,
                                 batch, seq, ctx_len, d)
        pool_l = _pool_lat_call(proj_l, box_lat, wpg, pscale, layer, batch, seq, d)
        merged_l = _merge_call(gla_l, pool_l, proj_l, w_gla_out, w_pool_out, layer, LAT_TM, 512)
        x_l, hf_l = _mm_ln_call(merged_l, w_out_b, x_l, mods, lnm_g, lnm_b, layer, 2,
                                (layer, 4, 3), lat_row_512, LN_TM, d, alpha)
        act_l = _ffn1_call(hf_l, w_ffn_in, layer, LAT_TM, 512)
        nxt = (layer + 1, 1, 0) if ctx_out else None
        x_l, h_l = _mm_ln_call(act_l, w_ffn_out_b, x_l, mods, lnf_g, lnf_b, layer, 5,
                               nxt, lat_row_512, LN_TM, ffn_tk, alpha)
        if ctx_out:
            pool_c = _pool_ctx_call(proj_c, band_ctx, icnt_ctx, wpg, pscale, layer, batch, ctx_len, d)
            merged_c = _merge_call(gla_c, pool_c, proj_c, w_gla_out, w_pool_out, layer, m_ctx, 512)
            x_c, hf_c = _mm_ln_call(merged_c, w_out_b, x_c, mods, lnm_g, lnm_b, layer, 2,
                                    (layer, 4, 3), ctx_row, m_ctx, d, alpha)
            act_c = _ffn1_call(hf_c, w_ffn_in, layer, m_ctx, 512)
            x_c, h_c = _mm_ln_call(act_c, w_ffn_out_b, x_c, mods, lnf_g, lnf_b, layer, 5,
                                   (layer + 1, 1, 0), ctx_row, m_ctx, ffn_tk, alpha)
    return x_l.reshape(batch, seq, d)
```

```python
import functools

import numpy as np
import jax
import jax.numpy as jnp
from jax import lax
from jax.experimental import pallas as pl
from jax.experimental.pallas import tpu as pltpu

F32 = jnp.float32
BF16 = jnp.bfloat16

GRID_W = 64
N_HEADS = 4
GATE_RANK = 16
GATE_NORM = 16.0
CHUNK = 64
POOL_WINDOWS = (2, 4, 8, 16)
N_MOD = 6
LN_EPS = 1e-5
RMS_EPS = 1e-6
LOG2E = 1.4426950408889634

LANES = 128
VMEM_LIMIT = 56 * 1024 * 1024

GLA_BLOCK = 512
GLA_FINAL_ROWS = 2048
GLA_NORM_ROWS = 256
POOL_TILE = 256
POOL_HALO_TILES = 2
LAT_TM = 1024
WIDE_TM = 2048
LN_TM = 512
LN_ROWS = 16

def _cparams(n_axes, vmem=VMEM_LIMIT):
    return pltpu.CompilerParams(dimension_semantics=("arbitrary",) * n_axes,
                                vmem_limit_bytes=vmem)


def _silu(v):
    return v * jax.nn.sigmoid(v)


def _mods_kernel(cond_ref, w_ref, b_ref, o_ref):
    a = _silu(cond_ref[...]).astype(BF16)
    o_ref[...] = jnp.dot(a, w_ref[...].astype(BF16), preferred_element_type=F32) + b_ref[...]


def _mods_call(cond, w_ada, b_ada):
    depth, d, n = w_ada.shape
    tn = 1024
    return pl.pallas_call(
        _mods_kernel,
        out_shape=jax.ShapeDtypeStruct((depth, 8, n), F32),
        grid=(depth, n // tn),
        in_specs=[pl.BlockSpec((8, d), lambda l, j: (0, 0)),
                  pl.BlockSpec((None, d, tn), lambda l, j: (l, 0, j)),
                  pl.BlockSpec((None, 1, tn), lambda l, j: (l, 0, j))],
        out_specs=pl.BlockSpec((None, 8, tn), lambda l, j: (l, 0, j)),
        compiler_params=_cparams(2),
        name="mods",
    )(cond, w_ada, b_ada.reshape(depth, 1, n))


def _mod_spec(layer, chunk, d):
    return pl.BlockSpec((None, 8, d), lambda *_: (layer, 0, chunk))


def _modulate_kernel(x_ref, sc_ref, sh_ref, o_ref, *, row_fn):
    r = row_fn(pl.program_id(0))
    sc = sc_ref[pl.ds(r, 1), :]
    sh = sh_ref[pl.ds(r, 1), :]
    o_ref[...] = (x_ref[...] * (1.0 + sc) + sh).astype(BF16)


def _modulate_call(x, mods, layer, row_fn, tm):
    m, d = x.shape
    return pl.pallas_call(
        functools.partial(_modulate_kernel, row_fn=row_fn),
        out_shape=jax.ShapeDtypeStruct((m, d), BF16),
        grid=(m // tm,),
        in_specs=[pl.BlockSpec((tm, d), lambda i: (i, 0)),
                  _mod_spec(layer, 1, d), _mod_spec(layer, 0, d)],
        out_specs=pl.BlockSpec((tm, d), lambda i: (i, 0)),
        compiler_params=_cparams(1),
        name="modulate",
    )(x, mods, mods)


def _in_proj_kernel(a_ref, wt_ref, gain_ref, o_ref, wb_ref, *, n_q, g_lo, g_hi, q_scale):
    j = pl.program_id(0)
    is_q = j < n_q
    is_g = jnp.logical_and(j >= g_lo, j < g_hi)

    is_plain = jnp.logical_not(jnp.logical_or(is_q, is_g))
    first = pl.program_id(1) == 0

    for cast in (True, False):
        def mm(cast=cast):
            if cast:
                wb = wt_ref[0].astype(BF16)
                wb_ref[...] = wb
            else:
                wb = wb_ref[...]
            return lax.dot_general(a_ref[...], wb, (((1,), (1,)), ((), ())),
                                   preferred_element_type=F32)

        step = first if cast else jnp.logical_not(first)

        @pl.when(jnp.logical_and(step, is_q))
        def _(mm=mm):
            o_ref[...] = (mm() * q_scale).astype(o_ref.dtype)

        @pl.when(jnp.logical_and(step, is_g))
        def _(mm=mm):
            o_ref[...] = (_silu(mm()) * gain_ref[...]).astype(o_ref.dtype)

        @pl.when(jnp.logical_and(step, is_plain))
        def _(mm=mm):
            o_ref[...] = mm().astype(o_ref.dtype)


def _in_proj_call(a, w_in_t, gain, layer, tm, tn, name):
    m, k = a.shape
    d = gain.shape[-1]
    d_key = d // 2
    n_head = 2 * d_key + 2 * d
    n = w_in_t.shape[1] - 2 * GATE_RANK
    n_a = n_head // tn
    g_lo = (2 * d_key + d) // tn
    g_hi = g_lo + d // tn
    kern = functools.partial(_in_proj_kernel, n_q=d_key // tn, g_lo=g_lo, g_hi=g_hi,
                             q_scale=(d_key // N_HEADS) ** -0.5)

    def row_start(j):
        sub = 8
        return (j * (tn // sub) + jnp.where(j >= n_a, 2 * GATE_RANK // sub, 0)) * sub

    return pl.pallas_call(
        kern,
        out_shape=jax.ShapeDtypeStruct((m, n), BF16),
        grid=(n // tn, m // tm),
        in_specs=[pl.BlockSpec((tm, k), lambda j, i: (i, 0)),
                  pl.BlockSpec((pl.Element(1), pl.Element(tn), pl.Element(k)),
                               lambda j, i: (layer, row_start(j), 0)),
                  pl.BlockSpec((None, 1, tn),
                               lambda j, i: (layer, 0, jnp.clip(j - g_lo, 0, g_hi - g_lo - 1)))],
        out_specs=pl.BlockSpec((tm, tn), lambda j, i: (i, j)),
        scratch_shapes=[pltpu.VMEM((tn, k), BF16)],
        compiler_params=_cparams(2),
        name=name,
    )(a, w_in_t, gain)


def _in_proj_decay_kernel(a_ref, wt_ref, o_ref, *, n_valid):
    y = lax.dot_general(a_ref[...], wt_ref[...].astype(BF16), (((1,), (1,)), ((), ())),
                        preferred_element_type=F32)
    lane = lax.broadcasted_iota(jnp.int32, y.shape, 1)
    o_ref[...] = jnp.where(lane < n_valid, y, 0.0)


def _in_proj_decay_call(a, w_in_t, layer, tm, d, name):
    m, k = a.shape
    row0 = 2 * (d // 2) + 2 * d
    assert row0 % LANES == 0
    return pl.pallas_call(
        functools.partial(_in_proj_decay_kernel, n_valid=2 * GATE_RANK),
        out_shape=jax.ShapeDtypeStruct((m, LANES), F32),
        grid=(m // tm,),
        in_specs=[pl.BlockSpec((tm, k), lambda i: (i, 0)),
                  pl.BlockSpec((None, LANES, k), lambda i: (layer, row0 // LANES, 0))],
        out_specs=pl.BlockSpec((tm, LANES), lambda i: (i, 0)),
        compiler_params=_cparams(1),
        name=name,
    )(a, w_in_t)


def _gla_kernel(qf_ref, kf_ref, vf_ref, af_ref, qb_ref, kb_ref, vb_ref, ab_ref, g_ref,
                qc_ref, kc_ref, vc_ref, ac_ref, gc_ref,
                wup_ref, bup_ref,
                out_ref, outc_ref,
                state_ref, acc_ref, accc_ref, *, nb):
    s = pl.program_id(2)
    fin_rows = out_ref.shape[0]

    row = lax.broadcasted_iota(jnp.int32, (CHUNK, CHUNK), 0)
    col = lax.broadcasted_iota(jnp.int32, (CHUNK, CHUNK), 1)
    keep = (row >= col, row <= col)
    row2 = lax.broadcasted_iota(jnp.int32, (CHUNK, 2 * CHUNK), 0)
    col2 = jnp.bitwise_and(lax.broadcasted_iota(jnp.int32, (CHUNK, 2 * CHUNK), 1), CHUNK - 1)
    keep2 = (row2 >= col2, row2 <= col2)
    ref_row = (CHUNK // 2 - 1, CHUNK // 2)
    last_row = (CHUNK - 1, 0)

    def log_decay(d, ar):
        z = jnp.dot(ar[...].astype(BF16), wup_ref[d], preferred_element_type=F32) + bup_ref[d]
        log_a = ((jnp.minimum(z, 0.0) - jnp.log(1.0 + jnp.exp2(jnp.abs(z) * (-LOG2E))))
                 * (LOG2E / GATE_NORM))
        hi = log_a.astype(BF16)
        return hi, (log_a - hi.astype(F32)).astype(BF16)

    def intra(d, tri, hi, lo, qr, kr, vr, c):
        sl = slice(c * CHUNK, (c + 1) * CHUNK)
        cum = jnp.dot(tri, jnp.concatenate([hi[sl], lo[sl]], axis=0), preferred_element_type=F32)
        ref = cum[ref_row[d]:ref_row[d] + 1, :]
        last = cum[last_row[d]:last_row[d] + 1, :]
        q = qr[sl, :].astype(F32)
        k = kr[sl, :].astype(F32)
        v = vr[sl, :]
        q_in = (q * jnp.exp2(cum - ref)).astype(BF16)
        k_in = (k * jnp.exp2(ref - cum)).astype(BF16)
        scores = lax.dot_general(q_in, k_in, (((1,), (1,)), ((), ())), preferred_element_type=F32)
        scores = jnp.where(keep[d], scores, 0.0).astype(BF16)
        o_intra = jnp.dot(scores, v, preferred_element_type=F32)
        q_inter = (q * jnp.exp2(cum)).astype(BF16)
        k_state = (k * jnp.exp2(last - cum)).astype(BF16)
        upd = lax.dot_general(v, k_state, (((0,), (0,)), ((), ())), preferred_element_type=F32)
        return o_intra, q_inter, upd, jnp.exp2(last)

    def scan_pair(fwd, bwd, acc, off_f, off_b):
        streams = []
        for d, (qr, kr, vr, ar), off in ((0, fwd, off_f), (1, bwd, off_b)):
            n_chunks = qr.shape[0] // CHUNK
            order = list(range(n_chunks - 1, -1, -1) if d == 1 else range(n_chunks))
            hi, lo = log_decay(d, ar)
            tri = jnp.where(keep2[d], 1.0, 0.0).astype(BF16)
            streams.append((d, tri, qr, kr, vr, hi, lo, order, off))
        n_steps = len(streams[0][7])
        pre = {}
        for i in range(n_steps):
            for d, tri, qr, kr, vr, hi, lo, order, _ in streams:
                pre[d, i] = intra(d, tri, hi, lo, qr, kr, vr, order[i])
        st = [state_ref[0], state_ref[1]]
        for i in range(n_steps):
            for d, _, _, _, _, _, _, order, off in streams:
                o_intra, q_inter, upd, decay = pre[d, i]
                o = o_intra + lax.dot_general(q_inter, st[d].astype(BF16), (((1,), (1,)), ((), ())),
                                              preferred_element_type=F32)
                st[d] = st[d] * decay + upd
                ra = pl.multiple_of(off + order[i] * CHUNK, CHUNK)
                acc[pl.ds(ra, CHUNK), :] += o
        state_ref[0] = st[0]
        state_ref[1] = st[1]

    def finalize(acc, base, gr, outr):
        rows = min(GLA_NORM_ROWS, outr.shape[0])

        def body(c, carry):
            r0 = pl.multiple_of(c * rows, rows)
            o = acc[pl.ds(pl.multiple_of(base + r0, rows), rows), :]
            ms = jnp.mean(o * o, axis=-1, keepdims=True)
            y = o * lax.rsqrt(ms + RMS_EPS) * gr[pl.ds(r0, rows), :].astype(F32)
            outr[pl.ds(r0, rows), :] = y.astype(outr.dtype)
            return carry

        lax.fori_loop(0, outr.shape[0] // rows, body, 0)

    ctx = (qc_ref, kc_ref, vc_ref, ac_ref)

    @pl.when(s == 0)
    def _():
        state_ref[...] = jnp.zeros_like(state_ref)
        acc_ref[...] = jnp.zeros_like(acc_ref)
        accc_ref[...] = jnp.zeros_like(accc_ref)
        scan_pair(ctx, ctx, accc_ref, 0, 0)

    @pl.when(s < nb)
    def _():
        scan_pair((qf_ref, kf_ref, vf_ref, af_ref), (qb_ref, kb_ref, vb_ref, ab_ref), acc_ref,
                  s * GLA_BLOCK, (nb - 1 - s) * GLA_BLOCK)

    @pl.when(s == nb)
    def _():
        finalize(accc_ref, 0, gc_ref, outc_ref)

    @pl.when(s >= nb)
    def _():
        finalize(acc_ref, (s - nb) * fin_rows, g_ref, out_ref)


def _gla_call(proj_l, alr_l, proj_c, alr_c, wup, bup, layer, batch, seq, ctx_len, d_model):
    d_key = d_model // 2
    head_k = d_key // N_HEADS
    head_v = d_model // N_HEADS
    nb = seq // GLA_BLOCK
    fin_rows = min(GLA_FINAL_ROWS, seq)
    nf = seq // fin_rows
    kq = d_key // head_k
    kv = (2 * d_key) // head_v
    kg = kv + d_model // head_v

    def blk(s, rev):
        return jnp.maximum(nb - 1 - s, 0) if rev else jnp.minimum(s, nb - 1)

    def lat(width, col0, rev):
        return pl.BlockSpec((GLA_BLOCK, width), lambda b, h, s: (b * nb + blk(s, rev), col0 + h))

    def lat_a(rev):
        return pl.BlockSpec((GLA_BLOCK, LANES), lambda b, h, s: (b * nb + blk(s, rev), 0))

    def fin(col0):
        return pl.BlockSpec((fin_rows, head_v),
                            lambda b, h, s: (b * nf + jnp.clip(s - nb, 0, nf - 1), col0 + h))

    def cx(width, col0):
        return pl.BlockSpec((ctx_len, width), lambda b, h, s: (b, col0 + h))

    in_specs = []
    for rev in (False, True):
        in_specs += [lat(head_k, 0, rev), lat(head_k, kq, rev), lat(head_v, kv, rev), lat_a(rev)]
    in_specs += [fin(kg),
                 cx(head_k, 0), cx(head_k, kq), cx(head_v, kv),
                 pl.BlockSpec((ctx_len, LANES), lambda b, h, s: (b, 0)),
                 cx(head_v, kg),
                 pl.BlockSpec((None, 2, LANES, head_k), lambda b, h, s: (layer, 0, 0, h)),
                 pl.BlockSpec((None, 2, 1, head_k), lambda b, h, s: (layer, 0, 0, h))]
    out_specs = [fin(0), cx(head_v, 0)]
    return pl.pallas_call(
        functools.partial(_gla_kernel, nb=nb),
        out_shape=(jax.ShapeDtypeStruct((batch * seq, d_model), BF16),
                   jax.ShapeDtypeStruct((batch * ctx_len, d_model), BF16)),
        grid=(batch, N_HEADS, nb + nf),
        in_specs=in_specs,
        out_specs=out_specs,
        scratch_shapes=[pltpu.VMEM((2, head_v, head_k), F32),
                        pltpu.VMEM((seq, head_v), F32),
                        pltpu.VMEM((ctx_len, head_v), F32)],
        compiler_params=_cparams(3),
        name="gla",
    )(proj_l, proj_l, proj_l, alr_l, proj_l, proj_l, proj_l, alr_l, proj_l,
      proj_c, proj_c, proj_c, alr_c, proj_c, wup, bup)


def _win(w):
    lo = w // 2
    return lo, w - lo - 1


def _band_matrix(n_tokens, w):
    lo, hi = _win(w)
    t = np.arange(n_tokens)
    off = t[None, :] - t[:, None]
    return ((off >= -lo) & (off <= hi)).astype(np.float32)


def _box_offsets(w):
    lo, hi = _win(w)
    rpt = POOL_TILE // GRID_W
    return list(range(-((lo + rpt - 1) // rpt), (hi + rpt - 1) // rpt + 1))


def _box_matrix(w, tile_offset):
    lo, hi = _win(w)
    rpt = POOL_TILE // GRID_W
    t = np.arange(POOL_TILE)
    a, c = t // GRID_W, t % GRID_W
    drow = rpt * tile_offset + a[None, :] - a[:, None]
    dcol = c[None, :] - c[:, None]
    return ((drow >= -lo) & (drow <= hi) & (dcol >= -lo) & (dcol <= hi)).astype(np.float32)


def _count(idx, n, w):
    lo, hi = _win(w)
    return jnp.minimum(idx + hi + 1, n) - jnp.maximum(idx - lo, 0)


def _pool_lat_kernel(p_ref, box_ref, wg_ref, sc_ref, o_ref, ps_ref, *, n_rows):
    g = pl.program_id(1)
    seq = n_rows * GRID_W
    halo = POOL_HALO_TILES * POOL_TILE
    n_tiles = seq // POOL_TILE
    zeros = jnp.zeros((halo, ps_ref.shape[1]), ps_ref.dtype)
    ps_ref[pl.ds(0, halo), :] = zeros
    ps_ref[pl.ds(halo + seq, halo), :] = zeros
    ps_ref[pl.ds(halo, seq), :] = p_ref[...]

    for gi, w in enumerate(POOL_WINDOWS):
        offsets = _box_offsets(w)

        @pl.when(g == gi)
        def _(w=w, offsets=offsets):
            def tile(t, carry):
                r0 = pl.multiple_of(t * POOL_TILE, POOL_TILE)
                tot = None
                for j, off in enumerate(offsets):
                    src = ps_ref[pl.ds(halo + r0 + off * POOL_TILE, POOL_TILE), :]
                    part = jnp.dot(box_ref[j], src, preferred_element_type=F32)
                    tot = part if tot is None else tot + part
                tok = r0 + lax.broadcasted_iota(jnp.int32, (POOL_TILE, 1), 0)
                img_row = lax.shift_right_logical(tok, GRID_W.bit_length() - 1)
                img_col = jnp.bitwise_and(tok, GRID_W - 1)
                cnt = _count(img_row, n_rows, w) * _count(img_col, GRID_W, w)
                pin = p_ref[pl.ds(r0, POOL_TILE), :].astype(F32)
                diff = (tot * (1.0 / cnt.astype(F32)) - pin).astype(BF16)
                y = jnp.dot(diff, wg_ref[...], preferred_element_type=F32) * sc_ref[...]
                o_ref[pl.ds(r0, POOL_TILE), :] = y.astype(o_ref.dtype)
                return carry

            lax.fori_loop(0, n_tiles, tile, 0, unroll=4)


def _pool_lat_call(proj_l, box, wpg, pscale, layer, batch, seq, d_model):
    d_pool = d_model // 2
    grp = d_pool // len(POOL_WINDOWS)
    col0 = (d_model // 2 * 2 + 2 * d_model) // grp
    halo = POOL_HALO_TILES * POOL_TILE
    return pl.pallas_call(
        functools.partial(_pool_lat_kernel, n_rows=seq // GRID_W),
        out_shape=jax.ShapeDtypeStruct((batch * seq, d_pool), BF16),
        grid=(batch, len(POOL_WINDOWS)),
        in_specs=[pl.BlockSpec((seq, grp), lambda b, g: (b, col0 + g)),
                  pl.BlockSpec((None, box.shape[1], POOL_TILE, POOL_TILE), lambda b, g: (g, 0, 0, 0)),
                  pl.BlockSpec((None, None, grp, grp), lambda b, g: (layer, g, 0, 0)),
                  pl.BlockSpec((None, 1, grp), lambda b, g: (layer, 0, g))],
        out_specs=pl.BlockSpec((seq, grp), lambda b, g: (b, g)),
        scratch_shapes=[pltpu.VMEM((seq + 2 * halo, grp), BF16)],
        compiler_params=_cparams(2),
        name="pool_lat",
    )(proj_l, box, wpg, pscale)


def _pool_ctx_kernel(p_ref, band_ref, icnt_ref, wg_ref, sc_ref, o_ref):
    tot = jnp.dot(band_ref[...], p_ref[...], preferred_element_type=F32)
    diff = (tot * icnt_ref[...] - p_ref[...].astype(F32)).astype(BF16)
    y = jnp.dot(diff, wg_ref[...], preferred_element_type=F32) * sc_ref[...]
    o_ref[...] = y.astype(o_ref.dtype)


def _pool_ctx_call(proj_c, band, icnt, wpg, pscale, layer, batch, ctx_len, d_model):
    d_pool = d_model // 2
    grp = d_pool // len(POOL_WINDOWS)
    col0 = (d_model // 2 * 2 + 2 * d_model) // grp
    return pl.pallas_call(
        _pool_ctx_kernel,
        out_shape=jax.ShapeDtypeStruct((batch * ctx_len, d_pool), BF16),
        grid=(batch, len(POOL_WINDOWS)),
        in_specs=[pl.BlockSpec((ctx_len, grp), lambda b, g: (b, col0 + g)),
                  pl.BlockSpec((None, ctx_len, ctx_len), lambda b, g: (g, 0, 0)),
                  pl.BlockSpec((None, ctx_len, 1), lambda b, g: (g, 0, 0)),
                  pl.BlockSpec((None, None, grp, grp), lambda b, g: (layer, g, 0, 0)),
                  pl.BlockSpec((None, 1, grp), lambda b, g: (layer, 0, g))],
        out_specs=pl.BlockSpec((ctx_len, grp), lambda b, g: (b, g)),
        compiler_params=_cparams(2),
        name="pool_ctx",
    )(proj_c, band, icnt, wpg, pscale)


def _merge_kernel(ga_ref, pa_ref, bgg_ref, bgp_ref, wg_ref, wp_ref, o_ref, wgb_ref, wpb_ref):
    first = pl.program_id(1) == 0

    def merge(wg, wp):
        yg = jnp.dot(ga_ref[...], wg, preferred_element_type=F32)
        yp = jnp.dot(pa_ref[...], wp, preferred_element_type=F32)
        y = (jax.nn.sigmoid(bgg_ref[...].astype(F32)) * yg
             + jax.nn.sigmoid(bgp_ref[...].astype(F32)) * yp)
        o_ref[...] = y.astype(o_ref.dtype)

    @pl.when(first)
    def _():
        wg = wg_ref[...].astype(BF16)
        wp = wp_ref[...].astype(BF16)
        wgb_ref[...] = wg
        wpb_ref[...] = wp
        merge(wg, wp)

    @pl.when(jnp.logical_not(first))
    def _():
        merge(wgb_ref[...], wpb_ref[...])


def _merge_call(gla_act, pool_act, proj, w_gla_out, w_pool_out, layer, tm, tn, n_rows=None):
    m = gla_act.shape[0] if n_rows is None else n_rows
    d = w_gla_out.shape[-1]
    d_pool = pool_act.shape[1]
    bg0 = (proj.shape[1] - 2 * d) // tn
    return pl.pallas_call(
        _merge_kernel,
        out_shape=jax.ShapeDtypeStruct((m, d), BF16),
        grid=(d // tn, m // tm),
        in_specs=[pl.BlockSpec((tm, d), lambda j, i: (i, 0)),
                  pl.BlockSpec((tm, d_pool), lambda j, i: (i, 0)),
                  pl.BlockSpec((tm, tn), lambda j, i: (i, bg0 + j)),
                  pl.BlockSpec((tm, tn), lambda j, i: (i, bg0 + d // tn + j)),
                  pl.BlockSpec((None, d, tn), lambda j, i: (layer, 0, j)),
                  pl.BlockSpec((None, d_pool, tn), lambda j, i: (layer, 0, j))],
        out_specs=pl.BlockSpec((tm, tn), lambda j, i: (i, j)),
        scratch_shapes=[pltpu.VMEM((d, tn), BF16), pltpu.VMEM((d_pool, tn), BF16)],
        compiler_params=_cparams(2),
        name="merge",
    )(gla_act, pool_act, proj, proj, w_gla_out, w_pool_out)


def _mm_ln_kernel(*refs, row_fn, nm, nk, alpha, emit_h):
    if emit_h:
        (a_ref, w_ref, x_ref, gt_ref, lng_ref, lnb_ref, sc_ref, sh_ref, xo_ref, ho_ref, *acc) = refs
    else:
        (a_ref, w_ref, x_ref, gt_ref, lng_ref, lnb_ref, xo_ref, *acc) = refs
    i = pl.program_id(0)
    kk = pl.program_id(1)
    tq = x_ref.shape[0] // nk

    def matmul(acc_ref):
        part = jnp.dot(a_ref[...], w_ref[...], preferred_element_type=F32)
        if nk == 1:
            acc_ref[...] = part
        else:
            acc_ref[...] += part

    def epilogue(acc_ref):
        base = kk * tq
        r = row_fn(i - 1)
        gt = gt_ref[pl.ds(r, 1), :]
        lng = lng_ref[...]
        lnb = lnb_ref[...]
        if emit_h:
            sc1 = 1.0 + sc_ref[pl.ds(r, 1), :]
            sh = sh_ref[pl.ds(r, 1), :]
        for c in range(tq // LN_ROWS):
            rows = pl.ds(pl.multiple_of(base + c * LN_ROWS, LN_ROWS), LN_ROWS)
            y = alpha * x_ref[rows, :] + gt * acc_ref[rows, :]
            mu = jnp.mean(y, axis=-1, keepdims=True)
            yc = y - mu
            var = jnp.mean(yc * yc, axis=-1, keepdims=True)
            xn = yc * lax.rsqrt(var + LN_EPS) * lng + lnb
            xo_ref[rows, :] = xn
            if emit_h:
                ho_ref[rows, :] = (xn * sc1 + sh).astype(BF16)
            if nk > 1:
                acc_ref[rows, :] = jnp.zeros((LN_ROWS, acc_ref.shape[1]), F32)

    @pl.when(i == 0)
    def _():
        if nk > 1:
            @pl.when(kk == 0)
            def _():
                acc[0][...] = jnp.zeros_like(acc[0])
                acc[1][...] = jnp.zeros_like(acc[1])
        matmul(acc[0])

    for parity in (0, 1):
        @pl.when(jnp.logical_and(jnp.logical_and(i >= 1, i < nm), i % 2 == parity))
        def _(parity=parity):
            epilogue(acc[1 - parity])
            matmul(acc[parity])

    @pl.when(i == nm)
    def _():
        epilogue(acc[(nm - 1) % 2])


def _mm_ln_call(a, w, x, mods, ln_gain, ln_bias, layer, gate_chunk, next_mod, row_fn, tm, tk, alpha,
                n_rows=None):
    m = x.shape[0] if n_rows is None else n_rows
    k = a.shape[1]
    d = x.shape[1]
    nk = k // tk
    nm = m // tm
    emit_h = next_mod is not None

    def k_idx(i, kk):
        return jnp.where(i < nm, kk, nk - 1)

    prev = lambda i, kk: (jnp.maximum(i - 1, 0), 0)
    in_specs = [pl.BlockSpec((tm, tk), lambda i, kk: (jnp.minimum(i, nm - 1), k_idx(i, kk))),
                pl.BlockSpec((None, tk, d), lambda i, kk: (layer, k_idx(i, kk), 0)),
                pl.BlockSpec((tm, d), prev),
                _mod_spec(layer, gate_chunk, d),
                pl.BlockSpec((None, 1, d), lambda i, kk: (layer, 0, 0)),
                pl.BlockSpec((None, 1, d), lambda i, kk: (layer, 0, 0))]
    args = [a, w, x, mods, ln_gain, ln_bias]
    out_shape = [jax.ShapeDtypeStruct((m, d), F32)]
    out_specs = [pl.BlockSpec((tm, d), prev)]
    if emit_h:
        nl, sc_chunk, sh_chunk = next_mod
        in_specs += [_mod_spec(nl, sc_chunk, d), _mod_spec(nl, sh_chunk, d)]
        args += [mods, mods]
        out_shape.append(jax.ShapeDtypeStruct((m, d), BF16))
        out_specs.append(pl.BlockSpec((tm, d), prev))
    outs = pl.pallas_call(
        functools.partial(_mm_ln_kernel, row_fn=row_fn, nm=nm, nk=nk, alpha=alpha, emit_h=emit_h),
        out_shape=out_shape,
        grid=(nm + 1, nk),
        in_specs=in_specs,
        out_specs=out_specs,
        scratch_shapes=[pltpu.VMEM((tm, d), F32), pltpu.VMEM((tm, d), F32)],
        compiler_params=_cparams(2),
        name="mm_ln",
    )(*args)
    return (outs[0], outs[1]) if emit_h else (outs[0], None)


def _ffn1_kernel(h_ref, wg_ref, wu_ref, o_ref, wgb_ref, wub_ref):
    first = pl.program_id(1) == 0

    def act(wg, wu):
        h = h_ref[...]
        gate = jnp.dot(h, wg, preferred_element_type=F32)
        up = jnp.dot(h, wu, preferred_element_type=F32)
        o_ref[...] = (_silu(gate) * up).astype(o_ref.dtype)

    @pl.when(first)
    def _():
        wg = wg_ref[...].astype(BF16)
        wu = wu_ref[...].astype(BF16)
        wgb_ref[...] = wg
        wub_ref[...] = wu
        act(wg, wu)

    @pl.when(jnp.logical_not(first))
    def _():
        act(wgb_ref[...], wub_ref[...])


def _ffn1_call(h, w_ffn_in, layer, tm, tn, n_rows=None):
    m = h.shape[0] if n_rows is None else n_rows
    d = h.shape[1]
    d_ff = w_ffn_in.shape[-1] // 2
    nj = d_ff // tn
    return pl.pallas_call(
        _ffn1_kernel,
        out_shape=jax.ShapeDtypeStruct((m, d_ff), BF16),
        grid=(nj, m // tm),
        in_specs=[pl.BlockSpec((tm, d), lambda j, i: (i, 0)),
                  pl.BlockSpec((None, d, tn), lambda j, i: (layer, 0, j)),
                  pl.BlockSpec((None, d, tn), lambda j, i: (layer, 0, nj + j))],
        out_specs=pl.BlockSpec((tm, tn), lambda j, i: (i, j)),
        scratch_shapes=[pltpu.VMEM((d, tn), BF16), pltpu.VMEM((d, tn), BF16)],
        compiler_params=_cparams(2),
        name="ffn1",
    )(h, w_ffn_in, w_ffn_in)


def kernel(x, c, ctx, c_ctx, w_ada, b_ada, w_in, w_decay_up, b_decay_up, gla_norm_gain, w_pool_group, pool_scale, w_gla_out, w_pool_out, w_out, ln_mix_gain, ln_mix_bias, w_ffn_in, w_ffn_out, ln_ffn_gain, ln_ffn_bias):
    batch, seq, d = x.shape
    ctx_len = ctx.shape[1]
    depth = w_ada.shape[0]
    d_key = d // 2
    d_pool = d // 2
    d_ff = w_ffn_out.shape[1]
    alpha = (2.0 * depth) ** 0.25
    m_lat = batch * seq
    m_ctx = batch * ctx_len
    assert seq % (2 * GLA_BLOCK) == 0 and seq % LAT_TM == 0 and ctx_len % CHUNK == 0
    assert batch + 1 <= 8

    x_l = x.reshape(m_lat, d)
    x_c = ctx.reshape(m_ctx, d)
    cond = jnp.zeros((8, d), F32).at[:batch].set(c).at[batch].set(c_ctx)
    w_in_t = jnp.swapaxes(w_in, 1, 2)
    wup = jnp.zeros((depth, 2, LANES, d_key), F32)
    wup = wup.at[:, 0, :GATE_RANK].set(w_decay_up[:, 0]).at[:, 1, GATE_RANK:2 * GATE_RANK].set(w_decay_up[:, 1])
    wup = wup.astype(BF16)
    bup = b_decay_up.reshape(depth, 2, 1, d_key)
    gain = gla_norm_gain.reshape(depth, 1, d)
    wpg = w_pool_group.astype(BF16)
    pscale = pool_scale.reshape(depth, 1, d_pool)
    w_out_b = w_out.astype(BF16)
    w_ffn_out_b = w_ffn_out.astype(BF16)
    lnm_g = ln_mix_gain.reshape(depth, 1, d)
    lnm_b = ln_mix_bias.reshape(depth, 1, d)
    lnf_g = ln_ffn_gain.reshape(depth, 1, d)
    lnf_b = ln_ffn_bias.reshape(depth, 1, d)
    n_off = max(len(_box_offsets(w)) for w in POOL_WINDOWS)
    assert max(abs(o) for w in POOL_WINDOWS for o in _box_offsets(w)) <= POOL_HALO_TILES
    box_np = np.zeros((len(POOL_WINDOWS), n_off, POOL_TILE, POOL_TILE), np.float32)
    for gi, w in enumerate(POOL_WINDOWS):
        for j, off in enumerate(_box_offsets(w)):
            box_np[gi, j] = _box_matrix(w, off)
    box_lat = jnp.asarray(box_np, BF16)
    band_ctx = jnp.asarray(np.stack([_band_matrix(ctx_len, w) for w in POOL_WINDOWS]), BF16)
    t = np.arange(ctx_len)
    icnt_ctx = jnp.asarray(np.stack([
        1.0 / (np.minimum(t + _win(w)[1] + 1, ctx_len) - np.maximum(t - _win(w)[0], 0))
        for w in POOL_WINDOWS]).astype(np.float32)[:, :, None])

    mods = _mods_call(cond, w_ada, b_ada)

    lat_row_1024 = lambda i: i // (seq // LAT_TM)
    lat_row_512 = lambda i: i // (seq // LN_TM)
    ctx_row = lambda i: batch

    h_l = _modulate_call(x_l, mods, 0, lat_row_1024, LAT_TM)
    h_c = _modulate_call(x_c, mods, 0, ctx_row, m_ctx)

    ffn_tk = d_ff // 4
    for layer in range(depth):
        ctx_out = layer < depth - 1
        proj_l = _in_proj_call(h_l, w_in_t, gain, layer, min(WIDE_TM, seq), 1024, "in_proj")
        alr_l = _in_proj_decay_call(h_l, w_in_t, layer, LAT_TM, d, "in_proj_decay")
        proj_c = _in_proj_call(h_c, w_in_t, gain, layer, m_ctx, 1024, "in_proj_ctx")
        alr_c = _in_proj_decay_call(h_c, w_in_t, layer, m_ctx, d, "in_proj_decay_ctx")
        gla_l, gla_c = _gla_call(proj_l, alr_l, proj_c, alr_c, wup, bup, layer,
                                 batch, seq, ctx_len, d)
        pool_l = _pool_lat_call(proj_l, box_lat, wpg, pscale, layer, batch, seq, d)
        merged_l = _merge_call(gla_l, pool_l, proj_l, w_gla_out, w_pool_out, layer, LN_TM, 1024)
        x_l, hf_l = _mm_ln_call(merged_l, w_out_b, x_l, mods, lnm_g, lnm_b, layer, 2,
                                (layer, 4, 3), lat_row_512, LN_TM, d, alpha)
        act_l = _ffn1_call(hf_l, w_ffn_in, layer, min(WIDE_TM, seq), 512)
        nxt = (layer + 1, 1, 0) if ctx_out else None
        x_l, h_l = _mm_ln_call(act_l, w_ffn_out_b, x_l, mods, lnf_g, lnf_b, layer, 5,
                               nxt, lat_row_512, LN_TM, ffn_tk, alpha)
        if ctx_out:
            pool_c = _pool_ctx_call(proj_c, band_ctx, icnt_ctx, wpg, pscale, layer, batch, ctx_len, d)
            merged_c = _merge_call(gla_c, pool_c, proj_c, w_gla_out, w_pool_out, layer, m_ctx, 1024)
            x_c, hf_c = _mm_ln_call(merged_c, w_out_b, x_c, mods, lnm_g, lnm_b, layer, 2,
                                    (layer, 4, 3), ctx_row, m_ctx, d, alpha)
            act_c = _ffn1_call(hf_c, w_ffn_in, layer, m_ctx, 512)
            x_c, h_c = _mm_ln_call(act_c, w_ffn_out_b, x_c, mods, lnf_g, lnf_b, layer, 5,
                                   (layer + 1, 1, 0), ctx_row, m_ctx, ffn_tk, alpha)
    return x_l.reshape(batch, seq, d)
```

```python
import functools

import numpy as np
import jax
import jax.numpy as jnp
from jax import lax
from jax.experimental import pallas as pl
from jax.experimental.pallas import tpu as pltpu

F32 = jnp.float32
BF16 = jnp.bfloat16

GRID_W = 64
N_HEADS = 4
GATE_RANK = 16
GATE_NORM = 16.0
CHUNK = 64
POOL_WINDOWS = (2, 4, 8, 16)
N_MOD = 6
LN_EPS = 1e-5
RMS_EPS = 1e-6
LOG2E = 1.4426950408889634

LANES = 128
VMEM_LIMIT = 56 * 1024 * 1024

GLA_BLOCK = 512
GLA_FINAL_ROWS = 2048
GLA_NORM_ROWS = 256
POOL_TILE = 256
POOL_HALO_TILES = 2
LAT_TM = 1024
LN_TM = 512
LN_ROWS = 16


def _cparams(n_axes, vmem=VMEM_LIMIT):
    return pltpu.CompilerParams(dimension_semantics=("arbitrary",) * n_axes,
                                vmem_limit_bytes=vmem)


def _silu(v):
    return v * jax.nn.sigmoid(v)


def _mods_kernel(cond_ref, w_ref, b_ref, o_ref):
    a = _silu(cond_ref[...]).astype(BF16)
    o_ref[...] = jnp.dot(a, w_ref[...].astype(BF16), preferred_element_type=F32) + b_ref[...]


def _mods_call(cond, w_ada, b_ada):
    depth, d, n = w_ada.shape
    tn = 1024
    return pl.pallas_call(
        _mods_kernel,
        out_shape=jax.ShapeDtypeStruct((depth, 8, n), F32),
        grid=(depth, n // tn),
        in_specs=[pl.BlockSpec((8, d), lambda l, j: (0, 0)),
                  pl.BlockSpec((None, d, tn), lambda l, j: (l, 0, j)),
                  pl.BlockSpec((None, 1, tn), lambda l, j: (l, 0, j))],
        out_specs=pl.BlockSpec((None, 8, tn), lambda l, j: (l, 0, j)),
        compiler_params=_cparams(2),
        name="mods",
    )(cond, w_ada, b_ada.reshape(depth, 1, n))


def _mod_spec(layer, chunk, d):
    return pl.BlockSpec((None, 8, d), lambda *_: (layer, 0, chunk))


def _modulate_kernel(x_ref, sc_ref, sh_ref, o_ref, *, row_fn):
    r = row_fn(pl.program_id(0))
    sc = sc_ref[pl.ds(r, 1), :]
    sh = sh_ref[pl.ds(r, 1), :]
    o_ref[...] = (x_ref[...] * (1.0 + sc) + sh).astype(BF16)


def _modulate_call(x, mods, layer, row_fn, tm):
    m, d = x.shape
    return pl.pallas_call(
        functools.partial(_modulate_kernel, row_fn=row_fn),
        out_shape=jax.ShapeDtypeStruct((m, d), BF16),
        grid=(m // tm,),
        in_specs=[pl.BlockSpec((tm, d), lambda i: (i, 0)),
                  _mod_spec(layer, 1, d), _mod_spec(layer, 0, d)],
        out_specs=pl.BlockSpec((tm, d), lambda i: (i, 0)),
        compiler_params=_cparams(1),
        name="modulate",
    )(x, mods, mods)


def _in_proj_kernel(a_ref, wt_ref, gain_ref, o_ref, wb_ref, *, n_q, g_lo, g_hi, q_scale):
    j = pl.program_id(0)
    is_q = j < n_q
    is_g = jnp.logical_and(j >= g_lo, j < g_hi)

    is_plain = jnp.logical_not(jnp.logical_or(is_q, is_g))
    first = pl.program_id(1) == 0

    for cast in (True, False):
        def mm(cast=cast):
            if cast:
                wb = wt_ref[0].astype(BF16)
                wb_ref[...] = wb
            else:
                wb = wb_ref[...]
            return lax.dot_general(a_ref[...], wb, (((1,), (1,)), ((), ())),
                                   preferred_element_type=F32)

        step = first if cast else jnp.logical_not(first)

        @pl.when(jnp.logical_and(step, is_q))
        def _(mm=mm):
            o_ref[...] = (mm() * q_scale).astype(o_ref.dtype)

        @pl.when(jnp.logical_and(step, is_g))
        def _(mm=mm):
            o_ref[...] = (_silu(mm()) * gain_ref[...]).astype(o_ref.dtype)

        @pl.when(jnp.logical_and(step, is_plain))
        def _(mm=mm):
            o_ref[...] = mm().astype(o_ref.dtype)


def _in_proj_call(a, w_in_t, gain, layer, tm, tn, name):
    m, k = a.shape
    d = gain.shape[-1]
    d_key = d // 2
    n_head = 2 * d_key + 2 * d
    n = w_in_t.shape[1] - 2 * GATE_RANK
    n_a = n_head // tn
    g_lo = (2 * d_key + d) // tn
    g_hi = g_lo + d // tn
    kern = functools.partial(_in_proj_kernel, n_q=d_key // tn, g_lo=g_lo, g_hi=g_hi,
                             q_scale=(d_key // N_HEADS) ** -0.5)

    def row_start(j):
        sub = 8
        return (j * (tn // sub) + jnp.where(j >= n_a, 2 * GATE_RANK // sub, 0)) * sub

    return pl.pallas_call(
        kern,
        out_shape=jax.ShapeDtypeStruct((m, n), BF16),
        grid=(n // tn, m // tm),
        in_specs=[pl.BlockSpec((tm, k), lambda j, i: (i, 0)),
                  pl.BlockSpec((pl.Element(1), pl.Element(tn), pl.Element(k)),
                               lambda j, i: (layer, row_start(j), 0)),
                  pl.BlockSpec((None, 1, tn),
                               lambda j, i: (layer, 0, jnp.clip(j - g_lo, 0, g_hi - g_lo - 1)))],
        out_specs=pl.BlockSpec((tm, tn), lambda j, i: (i, j)),
        scratch_shapes=[pltpu.VMEM((tn, k), BF16)],
        compiler_params=_cparams(2),
        name=name,
    )(a, w_in_t, gain)


def _in_proj_decay_kernel(a_ref, wt_ref, o_ref, *, n_valid):
    y = lax.dot_general(a_ref[...], wt_ref[...].astype(BF16), (((1,), (1,)), ((), ())),
                        preferred_element_type=F32)
    lane = lax.broadcasted_iota(jnp.int32, y.shape, 1)
    o_ref[...] = jnp.where(lane < n_valid, y, 0.0)


def _in_proj_decay_call(a, w_in_t, layer, tm, d, name):
    m, k = a.shape
    row0 = 2 * (d // 2) + 2 * d
    assert row0 % LANES == 0
    return pl.pallas_call(
        functools.partial(_in_proj_decay_kernel, n_valid=2 * GATE_RANK),
        out_shape=jax.ShapeDtypeStruct((m, LANES), F32),
        grid=(m // tm,),
        in_specs=[pl.BlockSpec((tm, k), lambda i: (i, 0)),
                  pl.BlockSpec((None, LANES, k), lambda i: (layer, row0 // LANES, 0))],
        out_specs=pl.BlockSpec((tm, LANES), lambda i: (i, 0)),
        compiler_params=_cparams(1),
        name=name,
    )(a, w_in_t)


def _gla_kernel(qf_ref, kf_ref, vf_ref, af_ref, qb_ref, kb_ref, vb_ref, ab_ref, g_ref,
                qc_ref, kc_ref, vc_ref, ac_ref, gc_ref,
                wup_ref, bup_ref,
                out_ref, outc_ref,
                state_ref, acc_ref, accc_ref, *, nb):
    s = pl.program_id(2)
    fin_rows = out_ref.shape[0]

    row = lax.broadcasted_iota(jnp.int32, (CHUNK, CHUNK), 0)
    col = lax.broadcasted_iota(jnp.int32, (CHUNK, CHUNK), 1)
    keep = (row >= col, row <= col)
    row2 = lax.broadcasted_iota(jnp.int32, (CHUNK, 2 * CHUNK), 0)
    col2 = jnp.bitwise_and(lax.broadcasted_iota(jnp.int32, (CHUNK, 2 * CHUNK), 1), CHUNK - 1)
    keep2 = (row2 >= col2, row2 <= col2)
    ref_row = (CHUNK // 2 - 1, CHUNK // 2)
    last_row = (CHUNK - 1, 0)

    def log_decay(d, ar):
        z = jnp.dot(ar[...].astype(BF16), wup_ref[d], preferred_element_type=F32) + bup_ref[d]
        log_a = ((jnp.minimum(z, 0.0) - jnp.log(1.0 + jnp.exp2(jnp.abs(z) * (-LOG2E))))
                 * (LOG2E / GATE_NORM))
        hi = log_a.astype(BF16)
        return hi, (log_a - hi.astype(F32)).astype(BF16)

    def intra(d, tri, hi, lo, qr, kr, vr, c):
        sl = slice(c * CHUNK, (c + 1) * CHUNK)
        cum = jnp.dot(tri, jnp.concatenate([hi[sl], lo[sl]], axis=0), preferred_element_type=F32)
        ref = cum[ref_row[d]:ref_row[d] + 1, :]
        last = cum[last_row[d]:last_row[d] + 1, :]
        q = qr[sl, :].astype(F32)
        k = kr[sl, :].astype(F32)
        v = vr[sl, :]
        q_in = (q * jnp.exp2(cum - ref)).astype(BF16)
        k_in = (k * jnp.exp2(ref - cum)).astype(BF16)
        scores = lax.dot_general(q_in, k_in, (((1,), (1,)), ((), ())), preferred_element_type=F32)
        scores = jnp.where(keep[d], scores, 0.0).astype(BF16)
        o_intra = jnp.dot(scores, v, preferred_element_type=F32)
        q_inter = (q * jnp.exp2(cum)).astype(BF16)
        k_state = (k * jnp.exp2(last - cum)).astype(BF16)
        upd = lax.dot_general(v, k_state, (((0,), (0,)), ((), ())), preferred_element_type=F32)
        return o_intra, q_inter, upd, jnp.exp2(last)

    def scan_pair(fwd, bwd, acc, off_f, off_b):
        streams = []
        for d, (qr, kr, vr, ar), off in ((0, fwd, off_f), (1, bwd, off_b)):
            n_chunks = qr.shape[0] // CHUNK
            order = list(range(n_chunks - 1, -1, -1) if d == 1 else range(n_chunks))
            hi, lo = log_decay(d, ar)
            tri = jnp.where(keep2[d], 1.0, 0.0).astype(BF16)
            streams.append((d, tri, qr, kr, vr, hi, lo, order, off))
        n_steps = len(streams[0][7])
        pre = {}
        for i in range(n_steps):
            for d, tri, qr, kr, vr, hi, lo, order, _ in streams:
                pre[d, i] = intra(d, tri, hi, lo, qr, kr, vr, order[i])
        st = [state_ref[0], state_ref[1]]
        for i in range(n_steps):
            for d, _, _, _, _, _, _, order, off in streams:
                o_intra, q_inter, upd, decay = pre[d, i]
                o = o_intra + lax.dot_general(q_inter, st[d].astype(BF16), (((1,), (1,)), ((), ())),
                                              preferred_element_type=F32)
                st[d] = st[d] * decay + upd
                ra = pl.multiple_of(off + order[i] * CHUNK, CHUNK)
                acc[pl.ds(ra, CHUNK), :] += o
        state_ref[0] = st[0]
        state_ref[1] = st[1]

    def finalize(acc, base, gr, outr):
        rows = min(GLA_NORM_ROWS, outr.shape[0])

        def body(c, carry):
            r0 = pl.multiple_of(c * rows, rows)
            o = acc[pl.ds(pl.multiple_of(base + r0, rows), rows), :]
            ms = jnp.mean(o * o, axis=-1, keepdims=True)
            y = o * lax.rsqrt(ms + RMS_EPS) * gr[pl.ds(r0, rows), :].astype(F32)
            outr[pl.ds(r0, rows), :] = y.astype(outr.dtype)
            return carry

        lax.fori_loop(0, outr.shape[0] // rows, body, 0)

    ctx = (qc_ref, kc_ref, vc_ref, ac_ref)

    @pl.when(s == 0)
    def _():
        state_ref[...] = jnp.zeros_like(state_ref)
        acc_ref[...] = jnp.zeros_like(acc_ref)
        accc_ref[...] = jnp.zeros_like(accc_ref)
        scan_pair(ctx, ctx, accc_ref, 0, 0)

    @pl.when(s < nb)
    def _():
        scan_pair((qf_ref, kf_ref, vf_ref, af_ref), (qb_ref, kb_ref, vb_ref, ab_ref), acc_ref,
                  s * GLA_BLOCK, (nb - 1 - s) * GLA_BLOCK)

    @pl.when(s == nb)
    def _():
        finalize(accc_ref, 0, gc_ref, outc_ref)

    @pl.when(s >= nb)
    def _():
        finalize(acc_ref, (s - nb) * fin_rows, g_ref, out_ref)


def _gla_call(proj_l, alr_l, proj_c, alr_c, wup, bup, layer, batch, seq, ctx_len, d_model):
    d_key = d_model // 2
    head_k = d_key // N_HEADS
    head_v = d_model // N_HEADS
    nb = seq // GLA_BLOCK
    fin_rows = min(GLA_FINAL_ROWS, seq)
    nf = seq // fin_rows
    kq = d_key // head_k
    kv = (2 * d_key) // head_v
    kg = kv + d_model // head_v

    def blk(s, rev):
        return jnp.maximum(nb - 1 - s, 0) if rev else jnp.minimum(s, nb - 1)

    def lat(width, col0, rev):
        return pl.BlockSpec((GLA_BLOCK, width), lambda b, h, s: (b * nb + blk(s, rev), col0 + h))

    def lat_a(rev):
        return pl.BlockSpec((GLA_BLOCK, LANES), lambda b, h, s: (b * nb + blk(s, rev), 0))

    def fin(col0):
        return pl.BlockSpec((fin_rows, head_v),
                            lambda b, h, s: (b * nf + jnp.clip(s - nb, 0, nf - 1), col0 + h))

    def cx(width, col0):
        return pl.BlockSpec((ctx_len, width), lambda b, h, s: (b, col0 + h))

    in_specs = []
    for rev in (False, True):
        in_specs += [lat(head_k, 0, rev), lat(head_k, kq, rev), lat(head_v, kv, rev), lat_a(rev)]
    in_specs += [fin(kg),
                 cx(head_k, 0), cx(head_k, kq), cx(head_v, kv),
                 pl.BlockSpec((ctx_len, LANES), lambda b, h, s: (b, 0)),
                 cx(head_v, kg),
                 pl.BlockSpec((None, 2, LANES, head_k), lambda b, h, s: (layer, 0, 0, h)),
                 pl.BlockSpec((None, 2, 1, head_k), lambda b, h, s: (layer, 0, 0, h))]
    out_specs = [fin(0), cx(head_v, 0)]
    return pl.pallas_call(
        functools.partial(_gla_kernel, nb=nb),
        out_shape=(jax.ShapeDtypeStruct((batch * seq, d_model), BF16),
                   jax.ShapeDtypeStruct((batch * ctx_len, d_model), BF16)),
        grid=(batch, N_HEADS, nb + nf),
        in_specs=in_specs,
        out_specs=out_specs,
        scratch_shapes=[pltpu.VMEM((2, head_v, head_k), F32),
                        pltpu.VMEM((seq, head_v), F32),
                        pltpu.VMEM((ctx_len, head_v), F32)],
        compiler_params=_cparams(3),
        name="gla",
    )(proj_l, proj_l, proj_l, alr_l, proj_l, proj_l, proj_l, alr_l, proj_l,
      proj_c, proj_c, proj_c, alr_c, proj_c, wup, bup)


def _win(w):
    lo = w // 2
    return lo, w - lo - 1


def _band_matrix(n_tokens, w):
    lo, hi = _win(w)
    t = np.arange(n_tokens)
    off = t[None, :] - t[:, None]
    return ((off >= -lo) & (off <= hi)).astype(np.float32)


def _box_offsets(w):
    lo, hi = _win(w)
    rpt = POOL_TILE // GRID_W
    return list(range(-((lo + rpt - 1) // rpt), (hi + rpt - 1) // rpt + 1))


def _box_matrix(w, tile_offset):
    lo, hi = _win(w)
    rpt = POOL_TILE // GRID_W
    t = np.arange(POOL_TILE)
    a, c = t // GRID_W, t % GRID_W
    drow = rpt * tile_offset + a[None, :] - a[:, None]
    dcol = c[None, :] - c[:, None]
    return ((drow >= -lo) & (drow <= hi) & (dcol >= -lo) & (dcol <= hi)).astype(np.float32)


def _count(idx, n, w):
    lo, hi = _win(w)
    return jnp.minimum(idx + hi + 1, n) - jnp.maximum(idx - lo, 0)


def _pool_lat_kernel(p_ref, box_ref, wg_ref, sc_ref, o_ref, ps_ref, *, n_rows):
    g = pl.program_id(1)
    seq = n_rows * GRID_W
    halo = POOL_HALO_TILES * POOL_TILE
    n_tiles = seq // POOL_TILE
    zeros = jnp.zeros((halo, ps_ref.shape[1]), ps_ref.dtype)
    ps_ref[pl.ds(0, halo), :] = zeros
    ps_ref[pl.ds(halo + seq, halo), :] = zeros
    ps_ref[pl.ds(halo, seq), :] = p_ref[...]

    for gi, w in enumerate(POOL_WINDOWS):
        offsets = _box_offsets(w)

        @pl.when(g == gi)
        def _(w=w, offsets=offsets):
            def tile(t, carry):
                r0 = pl.multiple_of(t * POOL_TILE, POOL_TILE)
                tot = None
                for j, off in enumerate(offsets):
                    src = ps_ref[pl.ds(halo + r0 + off * POOL_TILE, POOL_TILE), :]
                    part = jnp.dot(box_ref[j], src, preferred_element_type=F32)
                    tot = part if tot is None else tot + part
                tok = r0 + lax.broadcasted_iota(jnp.int32, (POOL_TILE, 1), 0)
                img_row = lax.shift_right_logical(tok, GRID_W.bit_length() - 1)
                img_col = jnp.bitwise_and(tok, GRID_W - 1)
                cnt = _count(img_row, n_rows, w) * _count(img_col, GRID_W, w)
                pin = p_ref[pl.ds(r0, POOL_TILE), :].astype(F32)
                diff = (tot * (1.0 / cnt.astype(F32)) - pin).astype(BF16)
                y = jnp.dot(diff, wg_ref[...], preferred_element_type=F32) * sc_ref[...]
                o_ref[pl.ds(r0, POOL_TILE), :] = y.astype(o_ref.dtype)
                return carry

            lax.fori_loop(0, n_tiles, tile, 0, unroll=4)


def _pool_lat_call(proj_l, box, wpg, pscale, layer, batch, seq, d_model):
    d_pool = d_model // 2
    grp = d_pool // len(POOL_WINDOWS)
    col0 = (d_model // 2 * 2 + 2 * d_model) // grp
    halo = POOL_HALO_TILES * POOL_TILE
    return pl.pallas_call(
        functools.partial(_pool_lat_kernel, n_rows=seq // GRID_W),
        out_shape=jax.ShapeDtypeStruct((batch * seq, d_pool), BF16),
        grid=(batch, len(POOL_WINDOWS)),
        in_specs=[pl.BlockSpec((seq, grp), lambda b, g: (b, col0 + g)),
                  pl.BlockSpec((None, box.shape[1], POOL_TILE, POOL_TILE), lambda b, g: (g, 0, 0, 0)),
                  pl.BlockSpec((None, None, grp, grp), lambda b, g: (layer, g, 0, 0)),
                  pl.BlockSpec((None, 1, grp), lambda b, g: (layer, 0, g))],
        out_specs=pl.BlockSpec((seq, grp), lambda b, g: (b, g)),
        scratch_shapes=[pltpu.VMEM((seq + 2 * halo, grp), BF16)],
        compiler_params=_cparams(2),
        name="pool_lat",
    )(proj_l, box, wpg, pscale)


def _pool_ctx_kernel(p_ref, band_ref, icnt_ref, wg_ref, sc_ref, o_ref):
    tot = jnp.dot(band_ref[...], p_ref[...], preferred_element_type=F32)
    diff = (tot * icnt_ref[...] - p_ref[...].astype(F32)).astype(BF16)
    y = jnp.dot(diff, wg_ref[...], preferred_element_type=F32) * sc_ref[...]
    o_ref[...] = y.astype(o_ref.dtype)


def _pool_ctx_call(proj_c, band, icnt, wpg, pscale, layer, batch, ctx_len, d_model):
    d_pool = d_model // 2
    grp = d_pool // len(POOL_WINDOWS)
    col0 = (d_model // 2 * 2 + 2 * d_model) // grp
    return pl.pallas_call(
        _pool_ctx_kernel,
        out_shape=jax.ShapeDtypeStruct((batch * ctx_len, d_pool), BF16),
        grid=(batch, len(POOL_WINDOWS)),
        in_specs=[pl.BlockSpec((ctx_len, grp), lambda b, g: (b, col0 + g)),
                  pl.BlockSpec((None, ctx_len, ctx_len), lambda b, g: (g, 0, 0)),
                  pl.BlockSpec((None, ctx_len, 1), lambda b, g: (g, 0, 0)),
                  pl.BlockSpec((None, None, grp, grp), lambda b, g: (layer, g, 0, 0)),
                  pl.BlockSpec((None, 1, grp), lambda b, g: (layer, 0, g))],
        out_specs=pl.BlockSpec((ctx_len, grp), lambda b, g: (b, g)),
        compiler_params=_cparams(2),
        name="pool_ctx",
    )(proj_c, band, icnt, wpg, pscale)


def _merge_kernel(ga_ref, pa_ref, bgg_ref, bgp_ref, wg_ref, wp_ref, o_ref, wgb_ref, wpb_ref):
    first = pl.program_id(1) == 0

    def merge(wg, wp):
        yg = jnp.dot(ga_ref[...], wg, preferred_element_type=F32)
        yp = jnp.dot(pa_ref[...], wp, preferred_element_type=F32)
        y = (jax.nn.sigmoid(bgg_ref[...].astype(F32)) * yg
             + jax.nn.sigmoid(bgp_ref[...].astype(F32)) * yp)
        o_ref[...] = y.astype(o_ref.dtype)

    @pl.when(first)
    def _():
        wg = wg_ref[...].astype(BF16)
        wp = wp_ref[...].astype(BF16)
        wgb_ref[...] = wg
        wpb_ref[...] = wp
        merge(wg, wp)

    @pl.when(jnp.logical_not(first))
    def _():
        merge(wgb_ref[...], wpb_ref[...])


def _merge_call(gla_act, pool_act, proj, w_gla_out, w_pool_out, layer, tm, tn, n_rows=None):
    m = gla_act.shape[0] if n_rows is None else n_rows
    d = w_gla_out.shape[-1]
    d_pool = pool_act.shape[1]
    bg0 = (proj.shape[1] - 2 * d) // tn
    return pl.pallas_call(
        _merge_kernel,
        out_shape=jax.ShapeDtypeStruct((m, d), BF16),
        grid=(d // tn, m // tm),
        in_specs=[pl.BlockSpec((tm, d), lambda j, i: (i, 0)),
                  pl.BlockSpec((tm, d_pool), lambda j, i: (i, 0)),
                  pl.BlockSpec((tm, tn), lambda j, i: (i, bg0 + j)),
                  pl.BlockSpec((tm, tn), lambda j, i: (i, bg0 + d // tn + j)),
                  pl.BlockSpec((None, d, tn), lambda j, i: (layer, 0, j)),
                  pl.BlockSpec((None, d_pool, tn), lambda j, i: (layer, 0, j))],
        out_specs=pl.BlockSpec((tm, tn), lambda j, i: (i, j)),
        scratch_shapes=[pltpu.VMEM((d, tn), BF16), pltpu.VMEM((d_pool, tn), BF16)],
        compiler_params=_cparams(2),
        name="merge",
    )(gla_act, pool_act, proj, proj, w_gla_out, w_pool_out)


def _mm_ln_kernel(*refs, row_fn, nm, nk, alpha, emit_h):
    if emit_h:
        (a_ref, w_ref, x_ref, gt_ref, lng_ref, lnb_ref, sc_ref, sh_ref, xo_ref, ho_ref, *acc) = refs
    else:
        (a_ref, w_ref, x_ref, gt_ref, lng_ref, lnb_ref, xo_ref, *acc) = refs
    i = pl.program_id(0)
    kk = pl.program_id(1)
    tq = x_ref.shape[0] // nk

    def matmul(acc_ref):
        part = jnp.dot(a_ref[...], w_ref[...], preferred_element_type=F32)
        if nk == 1:
            acc_ref[...] = part
        else:
            acc_ref[...] += part

    def epilogue(acc_ref):
        base = kk * tq
        r = row_fn(i - 1)
        gt = gt_ref[pl.ds(r, 1), :]
        lng = lng_ref[...]
        lnb = lnb_ref[...]
        if emit_h:
            sc1 = 1.0 + sc_ref[pl.ds(r, 1), :]
            sh = sh_ref[pl.ds(r, 1), :]
        for c in range(tq // LN_ROWS):
            rows = pl.ds(pl.multiple_of(base + c * LN_ROWS, LN_ROWS), LN_ROWS)
            y = alpha * x_ref[rows, :] + gt * acc_ref[rows, :]
            mu = jnp.mean(y, axis=-1, keepdims=True)
            yc = y - mu
            var = jnp.mean(yc * yc, axis=-1, keepdims=True)
            xn = yc * lax.rsqrt(var + LN_EPS) * lng + lnb
            xo_ref[rows, :] = xn
            if emit_h:
                ho_ref[rows, :] = (xn * sc1 + sh).astype(BF16)
            if nk > 1:
                acc_ref[rows, :] = jnp.zeros((LN_ROWS, acc_ref.shape[1]), F32)

    @pl.when(i == 0)
    def _():
        if nk > 1:
            @pl.when(kk == 0)
            def _():
                acc[0][...] = jnp.zeros_like(acc[0])
                acc[1][...] = jnp.zeros_like(acc[1])
        matmul(acc[0])

    for parity in (0, 1):
        @pl.when(jnp.logical_and(jnp.logical_and(i >= 1, i < nm), i % 2 == parity))
        def _(parity=parity):
            epilogue(acc[1 - parity])
            matmul(acc[parity])

    @pl.when(i == nm)
    def _():
        epilogue(acc[(nm - 1) % 2])


def _mm_ln_call(a, w, x, mods, ln_gain, ln_bias, layer, gate_chunk, next_mod, row_fn, tm, tk, alpha,
                n_rows=None):
    m = x.shape[0] if n_rows is None else n_rows
    k = a.shape[1]
    d = x.shape[1]
    nk = k // tk
    nm = m // tm
    emit_h = next_mod is not None

    def k_idx(i, kk):
        return jnp.where(i < nm, kk, nk - 1)

    prev = lambda i, kk: (jnp.maximum(i - 1, 0), 0)
    in_specs = [pl.BlockSpec((tm, tk), lambda i, kk: (jnp.minimum(i, nm - 1), k_idx(i, kk))),
                pl.BlockSpec((None, tk, d), lambda i, kk: (layer, k_idx(i, kk), 0)),
                pl.BlockSpec((tm, d), prev),
                _mod_spec(layer, gate_chunk, d),
                pl.BlockSpec((None, 1, d), lambda i, kk: (layer, 0, 0)),
                pl.BlockSpec((None, 1, d), lambda i, kk: (layer, 0, 0))]
    args = [a, w, x, mods, ln_gain, ln_bias]
    out_shape = [jax.ShapeDtypeStruct((m, d), F32)]
    out_specs = [pl.BlockSpec((tm, d), prev)]
    if emit_h:
        nl, sc_chunk, sh_chunk = next_mod
        in_specs += [_mod_spec(nl, sc_chunk, d), _mod_spec(nl, sh_chunk, d)]
        args += [mods, mods]
        out_shape.append(jax.ShapeDtypeStruct((m, d), BF16))
        out_specs.append(pl.BlockSpec((tm, d), prev))
    outs = pl.pallas_call(
        functools.partial(_mm_ln_kernel, row_fn=row_fn, nm=nm, nk=nk, alpha=alpha, emit_h=emit_h),
        out_shape=out_shape,
        grid=(nm + 1, nk),
        in_specs=in_specs,
        out_specs=out_specs,
        scratch_shapes=[pltpu.VMEM((tm, d), F32), pltpu.VMEM((tm, d), F32)],
        compiler_params=_cparams(2),
        name="mm_ln",
    )(*args)
    return (outs[0], outs[1]) if emit_h else (outs[0], None)


def _ffn1_kernel(h_ref, wg_ref, wu_ref, o_ref, wgb_ref, wub_ref):
    first = pl.program_id(1) == 0

    def act(wg, wu):
        h = h_ref[...]
        gate = jnp.dot(h, wg, preferred_element_type=F32)
        up = jnp.dot(h, wu, preferred_element_type=F32)
        o_ref[...] = (_silu(gate) * up).astype(o_ref.dtype)

    @pl.when(first)
    def _():
        wg = wg_ref[...].astype(BF16)
        wu = wu_ref[...].astype(BF16)
        wgb_ref[...] = wg
        wub_ref[...] = wu
        act(wg, wu)

    @pl.when(jnp.logical_not(first))
    def _():
        act(wgb_ref[...], wub_ref[...])


def _ffn1_call(h, w_ffn_in, layer, tm, tn, n_rows=None):
    m = h.shape[0] if n_rows is None else n_rows
    d = h.shape[1]
    d_ff = w_ffn_in.shape[-1] // 2
    nj = d_ff // tn
    return pl.pallas_call(
        _ffn1_kernel,
        out_shape=jax.ShapeDtypeStruct((m, d_ff), BF16),
        grid=(nj, m // tm),
        in_specs=[pl.BlockSpec((tm, d), lambda j, i: (i, 0)),
                  pl.BlockSpec((None, d, tn), lambda j, i: (layer, 0, j)),
                  pl.BlockSpec((None, d, tn), lambda j, i: (layer, 0, nj + j))],
        out_specs=pl.BlockSpec((tm, tn), lambda j, i: (i, j)),
        scratch_shapes=[pltpu.VMEM((d, tn), BF16), pltpu.VMEM((d, tn), BF16)],
        compiler_params=_cparams(2),
        name="ffn1",
    )(h, w_ffn_in, w_ffn_in)


def kernel(x, c, ctx, c_ctx, w_ada, b_ada, w_in, w_decay_up, b_decay_up, gla_norm_gain, w_pool_group, pool_scale, w_gla_out, w_pool_out, w_out, ln_mix_gain, ln_mix_bias, w_ffn_in, w_ffn_out, ln_ffn_gain, ln_ffn_bias):
    batch, seq, d = x.shape
    ctx_len = ctx.shape[1]
    depth = w_ada.shape[0]
    d_key = d // 2
    d_pool = d // 2
    d_ff = w_ffn_out.shape[1]
    alpha = (2.0 * depth) ** 0.25
    m_lat = batch * seq
    m_ctx = batch * ctx_len
    assert seq % (2 * GLA_BLOCK) == 0 and seq % LAT_TM == 0 and ctx_len % CHUNK == 0
    assert batch + 1 <= 8

    x_l = x.reshape(m_lat, d)
    x_c = ctx.reshape(m_ctx, d)
    cond = jnp.zeros((8, d), F32).at[:batch].set(c).at[batch].set(c_ctx)
    w_in_t = jnp.swapaxes(w_in, 1, 2)
    wup = jnp.zeros((depth, 2, LANES, d_key), F32)
    wup = wup.at[:, 0, :GATE_RANK].set(w_decay_up[:, 0]).at[:, 1, GATE_RANK:2 * GATE_RANK].set(w_decay_up[:, 1])
    wup = wup.astype(BF16)
    bup = b_decay_up.reshape(depth, 2, 1, d_key)
    gain = gla_norm_gain.reshape(depth, 1, d)
    wpg = w_pool_group.astype(BF16)
    pscale = pool_scale.reshape(depth, 1, d_pool)
    w_out_b = w_out.astype(BF16)
    w_ffn_out_b = w_ffn_out.astype(BF16)
    lnm_g = ln_mix_gain.reshape(depth, 1, d)
    lnm_b = ln_mix_bias.reshape(depth, 1, d)
    lnf_g = ln_ffn_gain.reshape(depth, 1, d)
    lnf_b = ln_ffn_bias.reshape(depth, 1, d)
    n_off = max(len(_box_offsets(w)) for w in POOL_WINDOWS)
    assert max(abs(o) for w in POOL_WINDOWS for o in _box_offsets(w)) <= POOL_HALO_TILES
    box_np = np.zeros((len(POOL_WINDOWS), n_off, POOL_TILE, POOL_TILE), np.float32)
    for gi, w in enumerate(POOL_WINDOWS):
        for j, off in enumerate(_box_offsets(w)):
            box_np[gi, j] = _box_matrix(w, off)
    box_lat = jnp.asarray(box_np, BF16)
    band_ctx = jnp.asarray(np.stack([_band_matrix(ctx_len, w) for w in POOL_WINDOWS]), BF16)
    t = np.arange(ctx_len)
    icnt_ctx = jnp.asarray(np.stack([
        1.0 / (np.minimum(t + _win(w)[1] + 1, ctx_len) - np.maximum(t - _win(w)[0], 0))
        for w in POOL_WINDOWS]).astype(np.float32)[:, :, None])

    mods = _mods_call(cond, w_ada, b_ada)

    lat_row_1024 = lambda i: i // (seq // LAT_TM)
    lat_row_512 = lambda i: i // (seq // LN_TM)
    ctx_row = lambda i: batch

    h_l = _modulate_call(x_l, mods, 0, lat_row_1024, LAT_TM)
    h_c = _modulate_call(x_c, mods, 0, ctx_row, m_ctx)

    ffn_tk = d_ff // 4
    for layer in range(depth):
        ctx_out = layer < depth - 1
        proj_l = _in_proj_call(h_l, w_in_t, gain, layer, LAT_TM, 1024, "in_proj")
        alr_l = _in_proj_decay_call(h_l, w_in_t, layer, LAT_TM, d, "in_proj_decay")
        proj_c = _in_proj_call(h_c, w_in_t, gain, layer, m_ctx, 1024, "in_proj_ctx")
        alr_c = _in_proj_decay_call(h_c, w_in_t, layer, m_ctx, d, "in_proj_decay_ctx")
        gla_l, gla_c = _gla_call(proj_l, alr_l, proj_c, alr_c, wup, bup, layer,
                                 batch, seq, ctx_len, d)
        pool_l = _pool_lat_call(proj_l, box_lat, wpg, pscale, layer, batch, seq, d)
        merged_l = _merge_call(gla_l, pool_l, proj_l, w_gla_out, w_pool_out, layer, LN_TM, 1024)
        x_l, hf_l = _mm_ln_call(merged_l, w_out_b, x_l, mods, lnm_g, lnm_b, layer, 2,
                                (layer, 4, 3), lat_row_512, LN_TM, d, alpha)
        act_l = _ffn1_call(hf_l, w_ffn_in, layer, LAT_TM, 512)
        nxt = (layer + 1, 1, 0) if ctx_out else None
        x_l, h_l = _mm_ln_call(act_l, w_ffn_out_b, x_l, mods, lnf_g, lnf_b, layer, 5,
                               nxt, lat_row_512, LN_TM, ffn_tk, alpha)
        if ctx_out:
            pool_c = _pool_ctx_call(proj_c, band_ctx, icnt_ctx, wpg, pscale, layer, batch, ctx_len, d)
            merged_c = _merge_call(gla_c, pool_c, proj_c, w_gla_out, w_pool_out, layer, m_ctx, 1024)
            x_c, hf_c = _mm_ln_call(merged_c, w_out_b, x_c, mods, lnm_g, lnm_b, layer, 2,
                                    (layer, 4, 3), ctx_row, m_ctx, d, alpha)
            act_c = _ffn1_call(hf_c, w_ffn_in, layer, m_ctx, 512)
            x_c, h_c = _mm_ln_call(act_c, w_ffn_out_b, x_c, mods, lnf_g, lnf_b, layer, 5,
                                   (layer + 1, 1, 0), ctx_row, m_ctx, ffn_tk, alpha)
    return x_l.reshape(batch, seq, d)
```

```python
import functools

import numpy as np
import jax
import jax.numpy as jnp
from jax import lax
from jax.experimental import pallas as pl
from jax.experimental.pallas import tpu as pltpu

F32 = jnp.float32
BF16 = jnp.bfloat16

GRID_W = 64
N_HEADS = 4
GATE_RANK = 16
GATE_NORM = 16.0
CHUNK = 64
POOL_WINDOWS = (2, 4, 8, 16)
N_MOD = 6
LN_EPS = 1e-5
RMS_EPS = 1e-6
LOG2E = 1.4426950408889634

LANES = 128
VMEM_LIMIT = 56 * 1024 * 1024

GLA_BLOCK = 512
GLA_GROUP = 4
GLA_FINAL_ROWS = 2048
GLA_NORM_ROWS = 256
POOL_TILE = 256
POOL_HALO_TILES = 2
LAT_TM = 1024
LN_TM = 512
LN_ROWS = 16


def _cparams(n_axes, vmem=VMEM_LIMIT):
    return pltpu.CompilerParams(dimension_semantics=("arbitrary",) * n_axes,
                                vmem_limit_bytes=vmem)


def _silu(v):
    return v * jax.nn.sigmoid(v)


def _mods_kernel(cond_ref, w_ref, b_ref, o_ref):
    a = _silu(cond_ref[...]).astype(BF16)
    o_ref[...] = jnp.dot(a, w_ref[...].astype(BF16), preferred_element_type=F32) + b_ref[...]


def _mods_call(cond, w_ada, b_ada):
    depth, d, n = w_ada.shape
    tn = 1024
    return pl.pallas_call(
        _mods_kernel,
        out_shape=jax.ShapeDtypeStruct((depth, 8, n), F32),
        grid=(depth, n // tn),
        in_specs=[pl.BlockSpec((8, d), lambda l, j: (0, 0)),
                  pl.BlockSpec((None, d, tn), lambda l, j: (l, 0, j)),
                  pl.BlockSpec((None, 1, tn), lambda l, j: (l, 0, j))],
        out_specs=pl.BlockSpec((None, 8, tn), lambda l, j: (l, 0, j)),
        compiler_params=_cparams(2),
        name="mods",
    )(cond, w_ada, b_ada.reshape(depth, 1, n))


def _mod_spec(layer, chunk, d):
    return pl.BlockSpec((None, 8, d), lambda *_: (layer, 0, chunk))


def _modulate_kernel(x_ref, sc_ref, sh_ref, o_ref, *, row_fn):
    r = row_fn(pl.program_id(0))
    sc = sc_ref[pl.ds(r, 1), :]
    sh = sh_ref[pl.ds(r, 1), :]
    o_ref[...] = (x_ref[...] * (1.0 + sc) + sh).astype(BF16)


def _modulate_call(x, mods, layer, row_fn, tm):
    m, d = x.shape
    return pl.pallas_call(
        functools.partial(_modulate_kernel, row_fn=row_fn),
        out_shape=jax.ShapeDtypeStruct((m, d), BF16),
        grid=(m // tm,),
        in_specs=[pl.BlockSpec((tm, d), lambda i: (i, 0)),
                  _mod_spec(layer, 1, d), _mod_spec(layer, 0, d)],
        out_specs=pl.BlockSpec((tm, d), lambda i: (i, 0)),
        compiler_params=_cparams(1),
        name="modulate",
    )(x, mods, mods)


def _in_proj_kernel(a_ref, wt_ref, gain_ref, o_ref, wb_ref, *, n_q, g_lo, g_hi, q_scale):
    j = pl.program_id(0)
    is_q = j < n_q
    is_g = jnp.logical_and(j >= g_lo, j < g_hi)

    is_plain = jnp.logical_not(jnp.logical_or(is_q, is_g))
    first = pl.program_id(1) == 0

    for cast in (True, False):
        def mm(cast=cast):
            if cast:
                wb = wt_ref[0].astype(BF16)
                wb_ref[...] = wb
            else:
                wb = wb_ref[...]
            return lax.dot_general(a_ref[...], wb, (((1,), (1,)), ((), ())),
                                   preferred_element_type=F32)

        step = first if cast else jnp.logical_not(first)

        @pl.when(jnp.logical_and(step, is_q))
        def _(mm=mm):
            o_ref[...] = (mm() * q_scale).astype(o_ref.dtype)

        @pl.when(jnp.logical_and(step, is_g))
        def _(mm=mm):
            o_ref[...] = (_silu(mm()) * gain_ref[...]).astype(o_ref.dtype)

        @pl.when(jnp.logical_and(step, is_plain))
        def _(mm=mm):
            o_ref[...] = mm().astype(o_ref.dtype)


def _in_proj_call(a, w_in_t, gain, layer, tm, tn, name):
    m, k = a.shape
    d = gain.shape[-1]
    d_key = d // 2
    n_head = 2 * d_key + 2 * d
    n = w_in_t.shape[1] - 2 * GATE_RANK
    n_a = n_head // tn
    g_lo = (2 * d_key + d) // tn
    g_hi = g_lo + d // tn
    kern = functools.partial(_in_proj_kernel, n_q=d_key // tn, g_lo=g_lo, g_hi=g_hi,
                             q_scale=(d_key // N_HEADS) ** -0.5)

    def row_start(j):
        sub = 8
        return (j * (tn // sub) + jnp.where(j >= n_a, 2 * GATE_RANK // sub, 0)) * sub

    return pl.pallas_call(
        kern,
        out_shape=jax.ShapeDtypeStruct((m, n), BF16),
        grid=(n // tn, m // tm),
        in_specs=[pl.BlockSpec((tm, k), lambda j, i: (i, 0)),
                  pl.BlockSpec((pl.Element(1), pl.Element(tn), pl.Element(k)),
                               lambda j, i: (layer, row_start(j), 0)),
                  pl.BlockSpec((None, 1, tn),
                               lambda j, i: (layer, 0, jnp.clip(j - g_lo, 0, g_hi - g_lo - 1)))],
        out_specs=pl.BlockSpec((tm, tn), lambda j, i: (i, j)),
        scratch_shapes=[pltpu.VMEM((tn, k), BF16)],
        compiler_params=_cparams(2),
        name=name,
    )(a, w_in_t, gain)


def _in_proj_decay_kernel(a_ref, wt_ref, o_ref, *, n_valid):
    y = lax.dot_general(a_ref[...], wt_ref[...].astype(BF16), (((1,), (1,)), ((), ())),
                        preferred_element_type=F32)
    lane = lax.broadcasted_iota(jnp.int32, y.shape, 1)
    o_ref[...] = jnp.where(lane < n_valid, y, 0.0)


def _in_proj_decay_call(a, w_in_t, layer, tm, d, name):
    m, k = a.shape
    row0 = 2 * (d // 2) + 2 * d
    assert row0 % LANES == 0
    return pl.pallas_call(
        functools.partial(_in_proj_decay_kernel, n_valid=2 * GATE_RANK),
        out_shape=jax.ShapeDtypeStruct((m, LANES), F32),
        grid=(m // tm,),
        in_specs=[pl.BlockSpec((tm, k), lambda i: (i, 0)),
                  pl.BlockSpec((None, LANES, k), lambda i: (layer, row0 // LANES, 0))],
        out_specs=pl.BlockSpec((tm, LANES), lambda i: (i, 0)),
        compiler_params=_cparams(1),
        name=name,
    )(a, w_in_t)


def _gla_kernel(qf_ref, kf_ref, vf_ref, af_ref, qb_ref, kb_ref, vb_ref, ab_ref, g_ref,
                qc_ref, kc_ref, vc_ref, ac_ref, gc_ref,
                wup_ref, bup_ref,
                out_ref, outc_ref,
                state_ref, acc_ref, accc_ref, *, nb):
    s = pl.program_id(2)
    fin_rows = out_ref.shape[0]

    row = lax.broadcasted_iota(jnp.int32, (CHUNK, CHUNK), 0)
    col = lax.broadcasted_iota(jnp.int32, (CHUNK, CHUNK), 1)
    keep = (row >= col, row <= col)
    row2 = lax.broadcasted_iota(jnp.int32, (CHUNK, 2 * CHUNK), 0)
    col2 = jnp.bitwise_and(lax.broadcasted_iota(jnp.int32, (CHUNK, 2 * CHUNK), 1), CHUNK - 1)
    keep2 = (row2 >= col2, row2 <= col2)
    ref_row = (CHUNK // 2 - 1, CHUNK // 2)
    last_row = (CHUNK - 1, 0)

    def log_decay(d, ar):
        z = jnp.dot(ar[...].astype(BF16), wup_ref[d], preferred_element_type=F32) + bup_ref[d]
        log_a = ((jnp.minimum(z, 0.0) - jnp.log(1.0 + jnp.exp2(jnp.abs(z) * (-LOG2E))))
                 * (LOG2E / GATE_NORM))
        hi = log_a.astype(BF16)
        return hi, (log_a - hi.astype(F32)).astype(BF16)

    def group_intra(d, tri, hi, lo, qr, kr, vr, g0):
        n = GLA_GROUP
        scan = list(range(n - 1, -1, -1) if d == 1 else range(n))
        cum, last, q, k, diag = {}, {}, {}, {}, {}
        for a in range(n):
            sl = slice((g0 + a) * CHUNK, (g0 + a + 1) * CHUNK)
            cum[a] = jnp.dot(tri, jnp.concatenate([hi[sl], lo[sl]], axis=0), preferred_element_type=F32)
            ref = cum[a][ref_row[d]:ref_row[d] + 1, :]
            last[a] = cum[a][last_row[d]:last_row[d] + 1, :]
            q[a] = qr[sl, :].astype(F32)
            k[a] = kr[sl, :].astype(F32)
            q_in = (q[a] * jnp.exp2(cum[a] - ref)).astype(BF16)
            k_in = (k[a] * jnp.exp2(ref - cum[a])).astype(BF16)
            s = lax.dot_general(q_in, k_in, (((1,), (1,)), ((), ())), preferred_element_type=F32)
            diag[a] = jnp.where(keep[d], s, 0.0)
        before = [None] * n
        after = [None] * n
        run = None
        for i in range(n):
            before[i] = run
            run = last[scan[i]] if run is None else run + last[scan[i]]
        total = run
        run = None
        for i in range(n - 1, -1, -1):
            after[i] = run
            run = last[scan[i]] if run is None else run + last[scan[i]]

        def scaled(x, log_factor):
            return (x if log_factor is None else x * jnp.exp2(log_factor)).astype(BF16)

        q_dec = {a: q[a] * jnp.exp2(cum[a]) for a in range(n)}
        k_dec = {a: k[a] * jnp.exp2(last[a] - cum[a]) for a in range(n)}
        k_plain = {a: k_dec[a].astype(BF16) for a in range(n)}
        zero_blk = jnp.zeros((CHUNK, CHUNK), F32)
        rows = [None] * n
        for i in range(n):
            ai = scan[i]
            blocks = {ai: diag[ai]}
            between = None
            for j in range(i - 1, -1, -1):
                aj = scan[j]
                blocks[aj] = lax.dot_general(scaled(q_dec[ai], between), k_plain[aj],
                                             (((1,), (1,)), ((), ())), preferred_element_type=F32)
                between = last[aj] if between is None else between + last[aj]
            rows[ai] = jnp.concatenate([blocks.get(a, zero_blk) for a in range(n)], axis=1).astype(BF16)
        p_mat = jnp.concatenate(rows, axis=0)
        q_state = jnp.concatenate([scaled(q_dec[a], before[scan.index(a)]) for a in range(n)], axis=0)
        k_state = jnp.concatenate([scaled(k_dec[a], after[scan.index(a)]) for a in range(n)], axis=0)
        v = vr[g0 * CHUNK:(g0 + n) * CHUNK, :]
        o_intra = jnp.dot(p_mat, v, preferred_element_type=F32)
        upd = lax.dot_general(v, k_state, (((0,), (0,)), ((), ())), preferred_element_type=F32)
        return o_intra, q_state, upd, jnp.exp2(total)

    def scan_pair(fwd, bwd, acc, off_f, off_b):
        streams = []
        for d, (qr, kr, vr, ar), off in ((0, fwd, off_f), (1, bwd, off_b)):
            n_groups = qr.shape[0] // (GLA_GROUP * CHUNK)
            order = list(range(n_groups - 1, -1, -1) if d == 1 else range(n_groups))
            hi, lo = log_decay(d, ar)
            tri = jnp.where(keep2[d], 1.0, 0.0).astype(BF16)
            streams.append((d, tri, qr, kr, vr, hi, lo, order, off))
        n_steps = len(streams[0][7])
        pre = {}
        st = [state_ref[0], state_ref[1]]
        rows = GLA_GROUP * CHUNK
        for i in range(n_steps):
            for d, tri, qr, kr, vr, hi, lo, order, _ in streams:
                pre[d, i] = group_intra(d, tri, hi, lo, qr, kr, vr, order[i] * GLA_GROUP)
            for d, _, _, _, _, _, _, order, off in streams:
                o_intra, q_state, upd, decay = pre[d, i]
                o = o_intra + lax.dot_general(q_state, st[d].astype(BF16), (((1,), (1,)), ((), ())),
                                              preferred_element_type=F32)
                st[d] = st[d] * decay + upd
                ra = pl.multiple_of(off + order[i] * rows, rows)
                acc[pl.ds(ra, rows), :] += o
        state_ref[0] = st[0]
        state_ref[1] = st[1]

    def finalize(acc, base, gr, outr):
        rows = min(GLA_NORM_ROWS, outr.shape[0])

        def body(c, carry):
            r0 = pl.multiple_of(c * rows, rows)
            o = acc[pl.ds(pl.multiple_of(base + r0, rows), rows), :]
            ms = jnp.mean(o * o, axis=-1, keepdims=True)
            y = o * lax.rsqrt(ms + RMS_EPS) * gr[pl.ds(r0, rows), :].astype(F32)
            outr[pl.ds(r0, rows), :] = y.astype(outr.dtype)
            return carry

        lax.fori_loop(0, outr.shape[0] // rows, body, 0)

    ctx = (qc_ref, kc_ref, vc_ref, ac_ref)

    @pl.when(s == 0)
    def _():
        state_ref[...] = jnp.zeros_like(state_ref)
        acc_ref[...] = jnp.zeros_like(acc_ref)
        accc_ref[...] = jnp.zeros_like(accc_ref)
        scan_pair(ctx, ctx, accc_ref, 0, 0)

    @pl.when(s < nb)
    def _():
        scan_pair((qf_ref, kf_ref, vf_ref, af_ref), (qb_ref, kb_ref, vb_ref, ab_ref), acc_ref,
                  s * GLA_BLOCK, (nb - 1 - s) * GLA_BLOCK)

    @pl.when(s == nb)
    def _():
        finalize(accc_ref, 0, gc_ref, outc_ref)

    @pl.when(s >= nb)
    def _():
        finalize(acc_ref, (s - nb) * fin_rows, g_ref, out_ref)


def _gla_call(proj_l, alr_l, proj_c, alr_c, wup, bup, layer, batch, seq, ctx_len, d_model):
    d_key = d_model // 2
    head_k = d_key // N_HEADS
    head_v = d_model // N_HEADS
    nb = seq // GLA_BLOCK
    fin_rows = min(GLA_FINAL_ROWS, seq)
    nf = seq // fin_rows
    kq = d_key // head_k
    kv = (2 * d_key) // head_v
    kg = kv + d_model // head_v

    def blk(s, rev):
        return jnp.maximum(nb - 1 - s, 0) if rev else jnp.minimum(s, nb - 1)

    def lat(width, col0, rev):
        return pl.BlockSpec((GLA_BLOCK, width), lambda b, h, s: (b * nb + blk(s, rev), col0 + h))

    def lat_a(rev):
        return pl.BlockSpec((GLA_BLOCK, LANES), lambda b, h, s: (b * nb + blk(s, rev), 0))

    def fin(col0):
        return pl.BlockSpec((fin_rows, head_v),
                            lambda b, h, s: (b * nf + jnp.clip(s - nb, 0, nf - 1), col0 + h))

    def cx(width, col0):
        return pl.BlockSpec((ctx_len, width), lambda b, h, s: (b, col0 + h))

    in_specs = []
    for rev in (False, True):
        in_specs += [lat(head_k, 0, rev), lat(head_k, kq, rev), lat(head_v, kv, rev), lat_a(rev)]
    in_specs += [fin(kg),
                 cx(head_k, 0), cx(head_k, kq), cx(head_v, kv),
                 pl.BlockSpec((ctx_len, LANES), lambda b, h, s: (b, 0)),
                 cx(head_v, kg),
                 pl.BlockSpec((None, 2, LANES, head_k), lambda b, h, s: (layer, 0, 0, h)),
                 pl.BlockSpec((None, 2, 1, head_k), lambda b, h, s: (layer, 0, 0, h))]
    out_specs = [fin(0), cx(head_v, 0)]
    return pl.pallas_call(
        functools.partial(_gla_kernel, nb=nb),
        out_shape=(jax.ShapeDtypeStruct((batch * seq, d_model), BF16),
                   jax.ShapeDtypeStruct((batch * ctx_len, d_model), BF16)),
        grid=(batch, N_HEADS, nb + nf),
        in_specs=in_specs,
        out_specs=out_specs,
        scratch_shapes=[pltpu.VMEM((2, head_v, head_k), F32),
                        pltpu.VMEM((seq, head_v), F32),
                        pltpu.VMEM((ctx_len, head_v), F32)],
        compiler_params=_cparams(3),
        name="gla",
    )(proj_l, proj_l, proj_l, alr_l, proj_l, proj_l, proj_l, alr_l, proj_l,
      proj_c, proj_c, proj_c, alr_c, proj_c, wup, bup)


def _win(w):
    lo = w // 2
    return lo, w - lo - 1


def _band_matrix(n_tokens, w):
    lo, hi = _win(w)
    t = np.arange(n_tokens)
    off = t[None, :] - t[:, None]
    return ((off >= -lo) & (off <= hi)).astype(np.float32)


def _box_offsets(w):
    lo, hi = _win(w)
    rpt = POOL_TILE // GRID_W
    return list(range(-((lo + rpt - 1) // rpt), (hi + rpt - 1) // rpt + 1))


def _box_matrix(w, tile_offset):
    lo, hi = _win(w)
    rpt = POOL_TILE // GRID_W
    t = np.arange(POOL_TILE)
    a, c = t // GRID_W, t % GRID_W
    drow = rpt * tile_offset + a[None, :] - a[:, None]
    dcol = c[None, :] - c[:, None]
    return ((drow >= -lo) & (drow <= hi) & (dcol >= -lo) & (dcol <= hi)).astype(np.float32)


def _count(idx, n, w):
    lo, hi = _win(w)
    return jnp.minimum(idx + hi + 1, n) - jnp.maximum(idx - lo, 0)


def _pool_lat_kernel(p_ref, box_ref, wg_ref, sc_ref, o_ref, ps_ref, *, n_rows):
    g = pl.program_id(1)
    seq = n_rows * GRID_W
    halo = POOL_HALO_TILES * POOL_TILE
    n_tiles = seq // POOL_TILE
    zeros = jnp.zeros((halo, ps_ref.shape[1]), ps_ref.dtype)
    ps_ref[pl.ds(0, halo), :] = zeros
    ps_ref[pl.ds(halo + seq, halo), :] = zeros
    ps_ref[pl.ds(halo, seq), :] = p_ref[...]

    for gi, w in enumerate(POOL_WINDOWS):
        offsets = _box_offsets(w)

        @pl.when(g == gi)
        def _(w=w, offsets=offsets):
            def tile(t, carry):
                r0 = pl.multiple_of(t * POOL_TILE, POOL_TILE)
                tot = None
                for j, off in enumerate(offsets):
                    src = ps_ref[pl.ds(halo + r0 + off * POOL_TILE, POOL_TILE), :]
                    part = jnp.dot(box_ref[j], src, preferred_element_type=F32)
                    tot = part if tot is None else tot + part
                tok = r0 + lax.broadcasted_iota(jnp.int32, (POOL_TILE, 1), 0)
                img_row = lax.shift_right_logical(tok, GRID_W.bit_length() - 1)
                img_col = jnp.bitwise_and(tok, GRID_W - 1)
                cnt = _count(img_row, n_rows, w) * _count(img_col, GRID_W, w)
                pin = p_ref[pl.ds(r0, POOL_TILE), :].astype(F32)
                diff = (tot * (1.0 / cnt.astype(F32)) - pin).astype(BF16)
                y = jnp.dot(diff, wg_ref[...], preferred_element_type=F32) * sc_ref[...]
                o_ref[pl.ds(r0, POOL_TILE), :] = y.astype(o_ref.dtype)
                return carry

            lax.fori_loop(0, n_tiles, tile, 0, unroll=4)


def _pool_lat_call(proj_l, box, wpg, pscale, layer, batch, seq, d_model):
    d_pool = d_model // 2
    grp = d_pool // len(POOL_WINDOWS)
    col0 = (d_model // 2 * 2 + 2 * d_model) // grp
    halo = POOL_HALO_TILES * POOL_TILE
    return pl.pallas_call(
        functools.partial(_pool_lat_kernel, n_rows=seq // GRID_W),
        out_shape=jax.ShapeDtypeStruct((batch * seq, d_pool), BF16),
        grid=(batch, len(POOL_WINDOWS)),
        in_specs=[pl.BlockSpec((seq, grp), lambda b, g: (b, col0 + g)),
                  pl.BlockSpec((None, box.shape[1], POOL_TILE, POOL_TILE), lambda b, g: (g, 0, 0, 0)),
                  pl.BlockSpec((None, None, grp, grp), lambda b, g: (layer, g, 0, 0)),
                  pl.BlockSpec((None, 1, grp), lambda b, g: (layer, 0, g))],
        out_specs=pl.BlockSpec((seq, grp), lambda b, g: (b, g)),
        scratch_shapes=[pltpu.VMEM((seq + 2 * halo, grp), BF16)],
        compiler_params=_cparams(2),
        name="pool_lat",
    )(proj_l, box, wpg, pscale)


def _pool_ctx_kernel(p_ref, band_ref, icnt_ref, wg_ref, sc_ref, o_ref):
    tot = jnp.dot(band_ref[...], p_ref[...], preferred_element_type=F32)
    diff = (tot * icnt_ref[...] - p_ref[...].astype(F32)).astype(BF16)
    y = jnp.dot(diff, wg_ref[...], preferred_element_type=F32) * sc_ref[...]
    o_ref[...] = y.astype(o_ref.dtype)


def _pool_ctx_call(proj_c, band, icnt, wpg, pscale, layer, batch, ctx_len, d_model):
    d_pool = d_model // 2
    grp = d_pool // len(POOL_WINDOWS)
    col0 = (d_model // 2 * 2 + 2 * d_model) // grp
    return pl.pallas_call(
        _pool_ctx_kernel,
        out_shape=jax.ShapeDtypeStruct((batch * ctx_len, d_pool), BF16),
        grid=(batch, len(POOL_WINDOWS)),
        in_specs=[pl.BlockSpec((ctx_len, grp), lambda b, g: (b, col0 + g)),
                  pl.BlockSpec((None, ctx_len, ctx_len), lambda b, g: (g, 0, 0)),
                  pl.BlockSpec((None, ctx_len, 1), lambda b, g: (g, 0, 0)),
                  pl.BlockSpec((None, None, grp, grp), lambda b, g: (layer, g, 0, 0)),
                  pl.BlockSpec((None, 1, grp), lambda b, g: (layer, 0, g))],
        out_specs=pl.BlockSpec((ctx_len, grp), lambda b, g: (b, g)),
        compiler_params=_cparams(2),
        name="pool_ctx",
    )(proj_c, band, icnt, wpg, pscale)


def _merge_kernel(ga_ref, pa_ref, bgg_ref, bgp_ref, wg_ref, wp_ref, o_ref, wgb_ref, wpb_ref):
    first = pl.program_id(1) == 0

    def merge(wg, wp):
        yg = jnp.dot(ga_ref[...], wg, preferred_element_type=F32)
        yp = jnp.dot(pa_ref[...], wp, preferred_element_type=F32)
        y = (jax.nn.sigmoid(bgg_ref[...].astype(F32)) * yg
             + jax.nn.sigmoid(bgp_ref[...].astype(F32)) * yp)
        o_ref[...] = y.astype(o_ref.dtype)

    @pl.when(first)
    def _():
        wg = wg_ref[...].astype(BF16)
        wp = wp_ref[...].astype(BF16)
        wgb_ref[...] = wg
        wpb_ref[...] = wp
        merge(wg, wp)

    @pl.when(jnp.logical_not(first))
    def _():
        merge(wgb_ref[...], wpb_ref[...])


def _merge_call(gla_act, pool_act, proj, w_gla_out, w_pool_out, layer, tm, tn, n_rows=None):
    m = gla_act.shape[0] if n_rows is None else n_rows
    d = w_gla_out.shape[-1]
    d_pool = pool_act.shape[1]
    bg0 = (proj.shape[1] - 2 * d) // tn
    return pl.pallas_call(
        _merge_kernel,
        out_shape=jax.ShapeDtypeStruct((m, d), BF16),
        grid=(d // tn, m // tm),
        in_specs=[pl.BlockSpec((tm, d), lambda j, i: (i, 0)),
                  pl.BlockSpec((tm, d_pool), lambda j, i: (i, 0)),
                  pl.BlockSpec((tm, tn), lambda j, i: (i, bg0 + j)),
                  pl.BlockSpec((tm, tn), lambda j, i: (i, bg0 + d // tn + j)),
                  pl.BlockSpec((None, d, tn), lambda j, i: (layer, 0, j)),
                  pl.BlockSpec((None, d_pool, tn), lambda j, i: (layer, 0, j))],
        out_specs=pl.BlockSpec((tm, tn), lambda j, i: (i, j)),
        scratch_shapes=[pltpu.VMEM((d, tn), BF16), pltpu.VMEM((d_pool, tn), BF16)],
        compiler_params=_cparams(2),
        name="merge",
    )(gla_act, pool_act, proj, proj, w_gla_out, w_pool_out)


def _mm_ln_kernel(*refs, row_fn, nm, nk, alpha, emit_h):
    if emit_h:
        (a_ref, w_ref, x_ref, gt_ref, lng_ref, lnb_ref, sc_ref, sh_ref, xo_ref, ho_ref, *acc) = refs
    else:
        (a_ref, w_ref, x_ref, gt_ref, lng_ref, lnb_ref, xo_ref, *acc) = refs
    i = pl.program_id(0)
    kk = pl.program_id(1)
    tq = x_ref.shape[0] // nk

    def matmul(acc_ref):
        part = jnp.dot(a_ref[...], w_ref[...], preferred_element_type=F32)
        if nk == 1:
            acc_ref[...] = part
        else:
            acc_ref[...] += part

    def epilogue(acc_ref):
        base = kk * tq
        r = row_fn(i - 1)
        gt = gt_ref[pl.ds(r, 1), :]
        lng = lng_ref[...]
        lnb = lnb_ref[...]
        if emit_h:
            sc1 = 1.0 + sc_ref[pl.ds(r, 1), :]
            sh = sh_ref[pl.ds(r, 1), :]
        for c in range(tq // LN_ROWS):
            rows = pl.ds(pl.multiple_of(base + c * LN_ROWS, LN_ROWS), LN_ROWS)
            y = alpha * x_ref[rows, :] + gt * acc_ref[rows, :]
            mu = jnp.mean(y, axis=-1, keepdims=True)
            yc = y - mu
            var = jnp.mean(yc * yc, axis=-1, keepdims=True)
            xn = yc * lax.rsqrt(var + LN_EPS) * lng + lnb
            xo_ref[rows, :] = xn
            if emit_h:
                ho_ref[rows, :] = (xn * sc1 + sh).astype(BF16)
            if nk > 1:
                acc_ref[rows, :] = jnp.zeros((LN_ROWS, acc_ref.shape[1]), F32)

    @pl.when(i == 0)
    def _():
        if nk > 1:
            @pl.when(kk == 0)
            def _():
                acc[0][...] = jnp.zeros_like(acc[0])
                acc[1][...] = jnp.zeros_like(acc[1])
        matmul(acc[0])

    for parity in (0, 1):
        @pl.when(jnp.logical_and(jnp.logical_and(i >= 1, i < nm), i % 2 == parity))
        def _(parity=parity):
            epilogue(acc[1 - parity])
            matmul(acc[parity])

    @pl.when(i == nm)
    def _():
        epilogue(acc[(nm - 1) % 2])


def _mm_ln_call(a, w, x, mods, ln_gain, ln_bias, layer, gate_chunk, next_mod, row_fn, tm, tk, alpha,
                n_rows=None):
    m = x.shape[0] if n_rows is None else n_rows
    k = a.shape[1]
    d = x.shape[1]
    nk = k // tk
    nm = m // tm
    emit_h = next_mod is not None

    def k_idx(i, kk):
        return jnp.where(i < nm, kk, nk - 1)

    prev = lambda i, kk: (jnp.maximum(i - 1, 0), 0)
    in_specs = [pl.BlockSpec((tm, tk), lambda i, kk: (jnp.minimum(i, nm - 1), k_idx(i, kk))),
                pl.BlockSpec((None, tk, d), lambda i, kk: (layer, k_idx(i, kk), 0)),
                pl.BlockSpec((tm, d), prev),
                _mod_spec(layer, gate_chunk, d),
                pl.BlockSpec((None, 1, d), lambda i, kk: (layer, 0, 0)),
                pl.BlockSpec((None, 1, d), lambda i, kk: (layer, 0, 0))]
    args = [a, w, x, mods, ln_gain, ln_bias]
    out_shape = [jax.ShapeDtypeStruct((m, d), F32)]
    out_specs = [pl.BlockSpec((tm, d), prev)]
    if emit_h:
        nl, sc_chunk, sh_chunk = next_mod
        in_specs += [_mod_spec(nl, sc_chunk, d), _mod_spec(nl, sh_chunk, d)]
        args += [mods, mods]
        out_shape.append(jax.ShapeDtypeStruct((m, d), BF16))
        out_specs.append(pl.BlockSpec((tm, d), prev))
    outs = pl.pallas_call(
        functools.partial(_mm_ln_kernel, row_fn=row_fn, nm=nm, nk=nk, alpha=alpha, emit_h=emit_h),
        out_shape=out_shape,
        grid=(nm + 1, nk),
        in_specs=in_specs,
        out_specs=out_specs,
        scratch_shapes=[pltpu.VMEM((tm, d), F32), pltpu.VMEM((tm, d), F32)],
        compiler_params=_cparams(2),
        name="mm_ln",
    )(*args)
    return (outs[0], outs[1]) if emit_h else (outs[0], None)


def _ffn1_kernel(h_ref, wg_ref, wu_ref, o_ref, wgb_ref, wub_ref):
    first = pl.program_id(1) == 0

    def act(wg, wu):
        h = h_ref[...]
        gate = jnp.dot(h, wg, preferred_element_type=F32)
        up = jnp.dot(h, wu, preferred_element_type=F32)
        o_ref[...] = (_silu(gate) * up).astype(o_ref.dtype)

    @pl.when(first)
    def _():
        wg = wg_ref[...].astype(BF16)
        wu = wu_ref[...].astype(BF16)
        wgb_ref[...] = wg
        wub_ref[...] = wu
        act(wg, wu)

    @pl.when(jnp.logical_not(first))
    def _():
        act(wgb_ref[...], wub_ref[...])


def _ffn1_call(h, w_ffn_in, layer, tm, tn, n_rows=None):
    m = h.shape[0] if n_rows is None else n_rows
    d = h.shape[1]
    d_ff = w_ffn_in.shape[-1] // 2
    nj = d_ff // tn
    return pl.pallas_call(
        _ffn1_kernel,
        out_shape=jax.ShapeDtypeStruct((m, d_ff), BF16),
        grid=(nj, m // tm),
        in_specs=[pl.BlockSpec((tm, d), lambda j, i: (i, 0)),
                  pl.BlockSpec((None, d, tn), lambda j, i: (layer, 0, j)),
                  pl.BlockSpec((None, d, tn), lambda j, i: (layer, 0, nj + j))],
        out_specs=pl.BlockSpec((tm, tn), lambda j, i: (i, j)),
        scratch_shapes=[pltpu.VMEM((d, tn), BF16), pltpu.VMEM((d, tn), BF16)],
        compiler_params=_cparams(2),
        name="ffn1",
    )(h, w_ffn_in, w_ffn_in)


def kernel(x, c, ctx, c_ctx, w_ada, b_ada, w_in, w_decay_up, b_decay_up, gla_norm_gain, w_pool_group, pool_scale, w_gla_out, w_pool_out, w_out, ln_mix_gain, ln_mix_bias, w_ffn_in, w_ffn_out, ln_ffn_gain, ln_ffn_bias):
    batch, seq, d = x.shape
    ctx_len = ctx.shape[1]
    depth = w_ada.shape[0]
    d_key = d // 2
    d_pool = d // 2
    d_ff = w_ffn_out.shape[1]
    alpha = (2.0 * depth) ** 0.25
    m_lat = batch * seq
    m_ctx = batch * ctx_len
    assert seq % (2 * GLA_BLOCK) == 0 and seq % LAT_TM == 0
    assert ctx_len % (GLA_GROUP * CHUNK) == 0 and GLA_BLOCK % (GLA_GROUP * CHUNK) == 0
    assert batch + 1 <= 8

    x_l = x.reshape(m_lat, d)
    x_c = ctx.reshape(m_ctx, d)
    cond = jnp.zeros((8, d), F32).at[:batch].set(c).at[batch].set(c_ctx)
    w_in_t = jnp.swapaxes(w_in, 1, 2)
    wup = jnp.zeros((depth, 2, LANES, d_key), F32)
    wup = wup.at[:, 0, :GATE_RANK].set(w_decay_up[:, 0]).at[:, 1, GATE_RANK:2 * GATE_RANK].set(w_decay_up[:, 1])
    wup = wup.astype(BF16)
    bup = b_decay_up.reshape(depth, 2, 1, d_key)
    gain = gla_norm_gain.reshape(depth, 1, d)
    wpg = w_pool_group.astype(BF16)
    pscale = pool_scale.reshape(depth, 1, d_pool)
    w_out_b = w_out.astype(BF16)
    w_ffn_out_b = w_ffn_out.astype(BF16)
    lnm_g = ln_mix_gain.reshape(depth, 1, d)
    lnm_b = ln_mix_bias.reshape(depth, 1, d)
    lnf_g = ln_ffn_gain.reshape(depth, 1, d)
    lnf_b = ln_ffn_bias.reshape(depth, 1, d)
    n_off = max(len(_box_offsets(w)) for w in POOL_WINDOWS)
    assert max(abs(o) for w in POOL_WINDOWS for o in _box_offsets(w)) <= POOL_HALO_TILES
    box_np = np.zeros((len(POOL_WINDOWS), n_off, POOL_TILE, POOL_TILE), np.float32)
    for gi, w in enumerate(POOL_WINDOWS):
        for j, off in enumerate(_box_offsets(w)):
            box_np[gi, j] = _box_matrix(w, off)
    box_lat = jnp.asarray(box_np, BF16)
    band_ctx = jnp.asarray(np.stack([_band_matrix(ctx_len, w) for w in POOL_WINDOWS]), BF16)
    t = np.arange(ctx_len)
    icnt_ctx = jnp.asarray(np.stack([
        1.0 / (np.minimum(t + _win(w)[1] + 1, ctx_len) - np.maximum(t - _win(w)[0], 0))
        for w in POOL_WINDOWS]).astype(np.float32)[:, :, None])

    mods = _mods_call(cond, w_ada, b_ada)

    lat_row_1024 = lambda i: i // (seq // LAT_TM)
    lat_row_512 = lambda i: i // (seq // LN_TM)
    ctx_row = lambda i: batch

    h_l = _modulate_call(x_l, mods, 0, lat_row_1024, LAT_TM)
    h_c = _modulate_call(x_c, mods, 0, ctx_row, m_ctx)

    ffn_tk = d_ff // 4
    for layer in range(depth):
        ctx_out = layer < depth - 1
        proj_l = _in_proj_call(h_l, w_in_t, gain, layer, LAT_TM, 1024, "in_proj")
        alr_l = _in_proj_decay_call(h_l, w_in_t, layer, LAT_TM, d, "in_proj_decay")
        proj_c = _in_proj_call(h_c, w_in_t, gain, layer, m_ctx, 1024, "in_proj_ctx")
        alr_c = _in_proj_decay_call(h_c, w_in_t, layer, m_ctx, d, "in_proj_decay_ctx")
        gla_l, gla_c = _gla_call(proj_l, alr_l, proj_c, alr_c, wup, bup, layer,
                                 batch, seq, ctx_len, d)
        pool_l = _pool_lat_call(proj_l, box_lat, wpg, pscale, layer, batch, seq, d)
        merged_l = _merge_call(gla_l, pool_l, proj_l, w_gla_out, w_pool_out, layer, LN_TM, 1024)
        x_l, hf_l = _mm_ln_call(merged_l, w_out_b, x_l, mods, lnm_g, lnm_b, layer, 2,
                                (layer, 4, 3), lat_row_512, LN_TM, d, alpha)
        act_l = _ffn1_call(hf_l, w_ffn_in, layer, LAT_TM, 512)
        nxt = (layer + 1, 1, 0) if ctx_out else None
        x_l, h_l = _mm_ln_call(act_l, w_ffn_out_b, x_l, mods, lnf_g, lnf_b, layer, 5,
                               nxt, lat_row_512, LN_TM, ffn_tk, alpha)
        if ctx_out:
            pool_c = _pool_ctx_call(proj_c, band_ctx, icnt_ctx, wpg, pscale, layer, batch, ctx_len, d)
            merged_c = _merge_call(gla_c, pool_c, proj_c, w_gla_out, w_pool_out, layer, m_ctx, 1024)
            x_c, hf_c = _mm_ln_call(merged_c, w_out_b, x_c, mods, lnm_g, lnm_b, layer, 2,
                                    (layer, 4, 3), ctx_row, m_ctx, d, alpha)
            act_c = _ffn1_call(hf_c, w_ffn_in, layer, m_ctx, 512)
            x_c, h_c = _mm_ln_call(act_c, w_ffn_out_b, x_c, mods, lnf_g, lnf_b, layer, 5,
                                   (layer + 1, 1, 0), ctx_row, m_ctx, ffn_tk, alpha)
    return x_l.reshape(batch, seq, d)
```

```python
import functools

import numpy as np
import jax
import jax.numpy as jnp
from jax import lax
from jax.experimental import pallas as pl
from jax.experimental.pallas import tpu as pltpu

F32 = jnp.float32
BF16 = jnp.bfloat16

GRID_W = 64
N_HEADS = 4
GATE_RANK = 16
GATE_NORM = 16.0
CHUNK = 64
POOL_WINDOWS = (2, 4, 8, 16)
N_MOD = 6
LN_EPS = 1e-5
RMS_EPS = 1e-6
LOG2E = 1.4426950408889634

LANES = 128
VMEM_LIMIT = 56 * 1024 * 1024

GLA_BLOCK = 512
GLA_GROUP = 4
GLA_FINAL_ROWS = 2048
GLA_NORM_ROWS = 256
POOL_TILE = 256
POOL_HALO_TILES = 2
LAT_TM = 1024
LN_TM = 512
LN_ROWS = 16


def _cparams(n_axes, vmem=VMEM_LIMIT):
    return pltpu.CompilerParams(dimension_semantics=("arbitrary",) * n_axes,
                                vmem_limit_bytes=vmem)


def _silu(v):
    return v * jax.nn.sigmoid(v)


def _mods_kernel(cond_ref, w_ref, b_ref, o_ref):
    a = _silu(cond_ref[...]).astype(BF16)
    o_ref[...] = jnp.dot(a, w_ref[...].astype(BF16), preferred_element_type=F32) + b_ref[...]


def _mods_call(cond, w_ada, b_ada):
    depth, d, n = w_ada.shape
    tn = 1024
    return pl.pallas_call(
        _mods_kernel,
        out_shape=jax.ShapeDtypeStruct((depth, 8, n), F32),
        grid=(depth, n // tn),
        in_specs=[pl.BlockSpec((8, d), lambda l, j: (0, 0)),
                  pl.BlockSpec((None, d, tn), lambda l, j: (l, 0, j)),
                  pl.BlockSpec((None, 1, tn), lambda l, j: (l, 0, j))],
        out_specs=pl.BlockSpec((None, 8, tn), lambda l, j: (l, 0, j)),
        compiler_params=_cparams(2),
        name="mods",
    )(cond, w_ada, b_ada.reshape(depth, 1, n))


def _mod_spec(layer, chunk, d):
    return pl.BlockSpec((None, 8, d), lambda *_: (layer, 0, chunk))


def _modulate_kernel(x_ref, sc_ref, sh_ref, o_ref, *, row_fn):
    r = row_fn(pl.program_id(0))
    sc = sc_ref[pl.ds(r, 1), :]
    sh = sh_ref[pl.ds(r, 1), :]
    o_ref[...] = (x_ref[...] * (1.0 + sc) + sh).astype(BF16)


def _modulate_call(x, mods, layer, row_fn, tm):
    m, d = x.shape
    return pl.pallas_call(
        functools.partial(_modulate_kernel, row_fn=row_fn),
        out_shape=jax.ShapeDtypeStruct((m, d), BF16),
        grid=(m // tm,),
        in_specs=[pl.BlockSpec((tm, d), lambda i: (i, 0)),
                  _mod_spec(layer, 1, d), _mod_spec(layer, 0, d)],
        out_specs=pl.BlockSpec((tm, d), lambda i: (i, 0)),
        compiler_params=_cparams(1),
        name="modulate",
    )(x, mods, mods)


def _in_proj_kernel(a_ref, ac_ref, wt_ref, gain_ref, o_ref, oc_ref, wb_ref, *, nm, n_q, g_lo, g_hi, q_scale):
    j = pl.program_id(0)
    i = pl.program_id(1)
    is_q = j < n_q
    is_g = jnp.logical_and(j >= g_lo, j < g_hi)
    is_plain = jnp.logical_not(jnp.logical_or(is_q, is_g))

    steps = (("cast", i == 0, a_ref, o_ref),
             ("reuse", jnp.logical_and(i > 0, i < nm), a_ref, o_ref),
             ("context", i == nm, ac_ref, oc_ref))
    for kind, step, src_ref, dst_ref in steps:
        def mm(kind=kind, src_ref=src_ref):
            if kind == "cast":
                wb = wt_ref[0].astype(BF16)
                wb_ref[...] = wb
            else:
                wb = wb_ref[...]
            return lax.dot_general(src_ref[...], wb, (((1,), (1,)), ((), ())),
                                   preferred_element_type=F32)

        @pl.when(jnp.logical_and(step, is_q))
        def _(mm=mm, dst_ref=dst_ref):
            dst_ref[...] = (mm() * q_scale).astype(dst_ref.dtype)

        @pl.when(jnp.logical_and(step, is_g))
        def _(mm=mm, dst_ref=dst_ref):
            dst_ref[...] = (_silu(mm()) * gain_ref[...]).astype(dst_ref.dtype)

        @pl.when(jnp.logical_and(step, is_plain))
        def _(mm=mm, dst_ref=dst_ref):
            dst_ref[...] = mm().astype(dst_ref.dtype)


def _in_proj_call(a, a_ctx, w_in_t, gain, layer, tm, tn):
    m, k = a.shape
    mc = a_ctx.shape[0]
    nm = m // tm
    d = gain.shape[-1]
    d_key = d // 2
    n_head = 2 * d_key + 2 * d
    n = w_in_t.shape[1] - 2 * GATE_RANK
    n_a = n_head // tn
    g_lo = (2 * d_key + d) // tn
    g_hi = g_lo + d // tn
    kern = functools.partial(_in_proj_kernel, nm=nm, n_q=d_key // tn, g_lo=g_lo, g_hi=g_hi,
                             q_scale=(d_key // N_HEADS) ** -0.5)

    def row_start(j):
        sub = 8
        return (j * (tn // sub) + jnp.where(j >= n_a, 2 * GATE_RANK // sub, 0)) * sub

    lat_row = lambda i: jnp.minimum(i, nm - 1)
    return pl.pallas_call(
        kern,
        out_shape=(jax.ShapeDtypeStruct((m, n), BF16), jax.ShapeDtypeStruct((mc, n), BF16)),
        grid=(n // tn, nm + 1),
        in_specs=[pl.BlockSpec((tm, k), lambda j, i: (lat_row(i), 0)),
                  pl.BlockSpec((mc, k), lambda j, i: (0, 0)),
                  pl.BlockSpec((pl.Element(1), pl.Element(tn), pl.Element(k)),
                               lambda j, i: (layer, row_start(j), 0)),
                  pl.BlockSpec((None, 1, tn),
                               lambda j, i: (layer, 0, jnp.clip(j - g_lo, 0, g_hi - g_lo - 1)))],
        out_specs=[pl.BlockSpec((tm, tn), lambda j, i: (lat_row(i), j)),
                   pl.BlockSpec((mc, tn), lambda j, i: (0, j))],
        scratch_shapes=[pltpu.VMEM((tn, k), BF16)],
        compiler_params=_cparams(2),
        name="in_proj",
    )(a, a_ctx, w_in_t, gain)


def _in_proj_decay_kernel(a_ref, wt_ref, o_ref, *, n_valid):
    y = lax.dot_general(a_ref[...], wt_ref[...].astype(BF16), (((1,), (1,)), ((), ())),
                        preferred_element_type=F32)
    lane = lax.broadcasted_iota(jnp.int32, y.shape, 1)
    o_ref[...] = jnp.where(lane < n_valid, y, 0.0)


def _in_proj_decay_call(a, w_in_t, layer, tm, d, name):
    m, k = a.shape
    row0 = 2 * (d // 2) + 2 * d
    assert row0 % LANES == 0
    return pl.pallas_call(
        functools.partial(_in_proj_decay_kernel, n_valid=2 * GATE_RANK),
        out_shape=jax.ShapeDtypeStruct((m, LANES), F32),
        grid=(m // tm,),
        in_specs=[pl.BlockSpec((tm, k), lambda i: (i, 0)),
                  pl.BlockSpec((None, LANES, k), lambda i: (layer, row0 // LANES, 0))],
        out_specs=pl.BlockSpec((tm, LANES), lambda i: (i, 0)),
        compiler_params=_cparams(1),
        name=name,
    )(a, w_in_t)


def _gla_kernel(qf_ref, kf_ref, vf_ref, af_ref, qb_ref, kb_ref, vb_ref, ab_ref, g_ref,
                qc_ref, kc_ref, vc_ref, ac_ref, gc_ref,
                wup_ref, bup_ref,
                out_ref, outc_ref,
                state_ref, acc_ref, accc_ref, *, nb):
    s = pl.program_id(2)
    fin_rows = out_ref.shape[0]

    row = lax.broadcasted_iota(jnp.int32, (CHUNK, CHUNK), 0)
    col = lax.broadcasted_iota(jnp.int32, (CHUNK, CHUNK), 1)
    keep = (row >= col, row <= col)
    row2 = lax.broadcasted_iota(jnp.int32, (CHUNK, 2 * CHUNK), 0)
    col2 = jnp.bitwise_and(lax.broadcasted_iota(jnp.int32, (CHUNK, 2 * CHUNK), 1), CHUNK - 1)
    keep2 = (row2 >= col2, row2 <= col2)
    ref_row = (CHUNK // 2 - 1, CHUNK // 2)
    last_row = (CHUNK - 1, 0)

    def log_decay(d, ar):
        z = jnp.dot(ar[...].astype(BF16), wup_ref[d], preferred_element_type=F32) + bup_ref[d]
        log_a = ((jnp.minimum(z, 0.0) - jnp.log(1.0 + jnp.exp2(jnp.abs(z) * (-LOG2E))))
                 * (LOG2E / GATE_NORM))
        hi = log_a.astype(BF16)
        return hi, (log_a - hi.astype(F32)).astype(BF16)

    def group_intra(d, tri, hi, lo, qr, kr, vr, g0):
        n = GLA_GROUP
        scan = list(range(n - 1, -1, -1) if d == 1 else range(n))
        cum, last, q, k, diag = {}, {}, {}, {}, {}
        for a in range(n):
            sl = slice((g0 + a) * CHUNK, (g0 + a + 1) * CHUNK)
            cum[a] = jnp.dot(tri, jnp.concatenate([hi[sl], lo[sl]], axis=0), preferred_element_type=F32)
            ref = cum[a][ref_row[d]:ref_row[d] + 1, :]
            last[a] = cum[a][last_row[d]:last_row[d] + 1, :]
            q[a] = qr[sl, :].astype(F32)
            k[a] = kr[sl, :].astype(F32)
            q_in = (q[a] * jnp.exp2(cum[a] - ref)).astype(BF16)
            k_in = (k[a] * jnp.exp2(ref - cum[a])).astype(BF16)
            s = lax.dot_general(q_in, k_in, (((1,), (1,)), ((), ())), preferred_element_type=F32)
            diag[a] = jnp.where(keep[d], s, 0.0)
        before = [None] * n
        after = [None] * n
        run = None
        for i in range(n):
            before[i] = run
            run = last[scan[i]] if run is None else run + last[scan[i]]
        total = run
        run = None
        for i in range(n - 1, -1, -1):
            after[i] = run
            run = last[scan[i]] if run is None else run + last[scan[i]]

        def scaled(x, log_factor):
            return (x if log_factor is None else x * jnp.exp2(log_factor)).astype(BF16)

        q_dec = {a: q[a] * jnp.exp2(cum[a]) for a in range(n)}
        k_dec = {a: k[a] * jnp.exp2(last[a] - cum[a]) for a in range(n)}
        k_plain = {a: k_dec[a].astype(BF16) for a in range(n)}
        zero_blk = jnp.zeros((CHUNK, CHUNK), F32)
        rows = [None] * n
        for i in range(n):
            ai = scan[i]
            blocks = {ai: diag[ai]}
            between = None
            for j in range(i - 1, -1, -1):
                aj = scan[j]
                blocks[aj] = lax.dot_general(scaled(q_dec[ai], between), k_plain[aj],
                                             (((1,), (1,)), ((), ())), preferred_element_type=F32)
                between = last[aj] if between is None else between + last[aj]
            rows[ai] = jnp.concatenate([blocks.get(a, zero_blk) for a in range(n)], axis=1).astype(BF16)
        p_mat = jnp.concatenate(rows, axis=0)
        q_state = jnp.concatenate([scaled(q_dec[a], before[scan.index(a)]) for a in range(n)], axis=0)
        k_state = jnp.concatenate([scaled(k_dec[a], after[scan.index(a)]) for a in range(n)], axis=0)
        v = vr[g0 * CHUNK:(g0 + n) * CHUNK, :]
        o_intra = jnp.dot(p_mat, v, preferred_element_type=F32)
        upd = lax.dot_general(v, k_state, (((0,), (0,)), ((), ())), preferred_element_type=F32)
        return o_intra, q_state, upd, jnp.exp2(total)

    def scan_pair(fwd, bwd, acc, off_f, off_b):
        streams = []
        for d, (qr, kr, vr, ar), off in ((0, fwd, off_f), (1, bwd, off_b)):
            n_groups = qr.shape[0] // (GLA_GROUP * CHUNK)
            order = list(range(n_groups - 1, -1, -1) if d == 1 else range(n_groups))
            hi, lo = log_decay(d, ar)
            tri = jnp.where(keep2[d], 1.0, 0.0).astype(BF16)
            streams.append((d, tri, qr, kr, vr, hi, lo, order, off))
        n_steps = len(streams[0][7])
        pre = {}
        st = [state_ref[0], state_ref[1]]
        rows = GLA_GROUP * CHUNK
        for i in range(n_steps):
            for d, tri, qr, kr, vr, hi, lo, order, _ in streams:
                pre[d, i] = group_intra(d, tri, hi, lo, qr, kr, vr, order[i] * GLA_GROUP)
            for d, _, _, _, _, _, _, order, off in streams:
                o_intra, q_state, upd, decay = pre[d, i]
                o = o_intra + lax.dot_general(q_state, st[d].astype(BF16), (((1,), (1,)), ((), ())),
                                              preferred_element_type=F32)
                st[d] = st[d] * decay + upd
                ra = pl.multiple_of(off + order[i] * rows, rows)
                acc[pl.ds(ra, rows), :] += o
        state_ref[0] = st[0]
        state_ref[1] = st[1]

    def finalize(acc, base, gr, outr):
        rows = min(GLA_NORM_ROWS, outr.shape[0])

        def body(c, carry):
            r0 = pl.multiple_of(c * rows, rows)
            o = acc[pl.ds(pl.multiple_of(base + r0, rows), rows), :]
            ms = jnp.mean(o * o, axis=-1, keepdims=True)
            y = o * lax.rsqrt(ms + RMS_EPS) * gr[pl.ds(r0, rows), :].astype(F32)
            outr[pl.ds(r0, rows), :] = y.astype(outr.dtype)
            return carry

        lax.fori_loop(0, outr.shape[0] // rows, body, 0)

    ctx = (qc_ref, kc_ref, vc_ref, ac_ref)

    @pl.when(s == 0)
    def _():
        state_ref[...] = jnp.zeros_like(state_ref)
        acc_ref[...] = jnp.zeros_like(acc_ref)
        accc_ref[...] = jnp.zeros_like(accc_ref)
        scan_pair(ctx, ctx, accc_ref, 0, 0)

    @pl.when(s < nb)
    def _():
        scan_pair((qf_ref, kf_ref, vf_ref, af_ref), (qb_ref, kb_ref, vb_ref, ab_ref), acc_ref,
                  s * GLA_BLOCK, (nb - 1 - s) * GLA_BLOCK)

    @pl.when(s == nb)
    def _():
        finalize(accc_ref, 0, gc_ref, outc_ref)

    @pl.when(s >= nb)
    def _():
        finalize(acc_ref, (s - nb) * fin_rows, g_ref, out_ref)


def _gla_call(proj_l, alr_l, proj_c, alr_c, wup, bup, layer, batch, seq, ctx_len, d_model):
    d_key = d_model // 2
    head_k = d_key // N_HEADS
    head_v = d_model // N_HEADS
    nb = seq // GLA_BLOCK
    fin_rows = min(GLA_FINAL_ROWS, seq)
    nf = seq // fin_rows
    kq = d_key // head_k
    kv = (2 * d_key) // head_v
    kg = kv + d_model // head_v

    def blk(s, rev):
        return jnp.maximum(nb - 1 - s, 0) if rev else jnp.minimum(s, nb - 1)

    def lat(width, col0, rev):
        return pl.BlockSpec((GLA_BLOCK, width), lambda b, h, s: (b * nb + blk(s, rev), col0 + h))

    def lat_a(rev):
        return pl.BlockSpec((GLA_BLOCK, LANES), lambda b, h, s: (b * nb + blk(s, rev), 0))

    def fin(col0):
        return pl.BlockSpec((fin_rows, head_v),
                            lambda b, h, s: (b * nf + jnp.clip(s - nb, 0, nf - 1), col0 + h))

    def cx(width, col0):
        return pl.BlockSpec((ctx_len, width), lambda b, h, s: (b, col0 + h))

    in_specs = []
    for rev in (False, True):
        in_specs += [lat(head_k, 0, rev), lat(head_k, kq, rev), lat(head_v, kv, rev), lat_a(rev)]
    in_specs += [fin(kg),
                 cx(head_k, 0), cx(head_k, kq), cx(head_v, kv),
                 pl.BlockSpec((ctx_len, LANES), lambda b, h, s: (b, 0)),
                 cx(head_v, kg),
                 pl.BlockSpec((None, 2, LANES, head_k), lambda b, h, s: (layer, 0, 0, h)),
                 pl.BlockSpec((None, 2, 1, head_k), lambda b, h, s: (layer, 0, 0, h))]
    out_specs = [fin(0), cx(head_v, 0)]
    return pl.pallas_call(
        functools.partial(_gla_kernel, nb=nb),
        out_shape=(jax.ShapeDtypeStruct((batch * seq, d_model), BF16),
                   jax.ShapeDtypeStruct((batch * ctx_len, d_model), BF16)),
        grid=(batch, N_HEADS, nb + nf),
        in_specs=in_specs,
        out_specs=out_specs,
        scratch_shapes=[pltpu.VMEM((2, head_v, head_k), F32),
                        pltpu.VMEM((seq, head_v), F32),
                        pltpu.VMEM((ctx_len, head_v), F32)],
        compiler_params=_cparams(3),
        name="gla",
    )(proj_l, proj_l, proj_l, alr_l, proj_l, proj_l, proj_l, alr_l, proj_l,
      proj_c, proj_c, proj_c, alr_c, proj_c, wup, bup)


def _win(w):
    lo = w // 2
    return lo, w - lo - 1


def _band_matrix(n_tokens, w):
    lo, hi = _win(w)
    t = np.arange(n_tokens)
    off = t[None, :] - t[:, None]
    return ((off >= -lo) & (off <= hi)).astype(np.float32)


def _box_offsets(w):
    lo, hi = _win(w)
    rpt = POOL_TILE // GRID_W
    return list(range(-((lo + rpt - 1) // rpt), (hi + rpt - 1) // rpt + 1))


def _box_matrix(w, tile_offset):
    lo, hi = _win(w)
    rpt = POOL_TILE // GRID_W
    t = np.arange(POOL_TILE)
    a, c = t // GRID_W, t % GRID_W
    drow = rpt * tile_offset + a[None, :] - a[:, None]
    dcol = c[None, :] - c[:, None]
    return ((drow >= -lo) & (drow <= hi) & (dcol >= -lo) & (dcol <= hi)).astype(np.float32)


def _count(idx, n, w):
    lo, hi = _win(w)
    return jnp.minimum(idx + hi + 1, n) - jnp.maximum(idx - lo, 0)


def _pool_lat_kernel(p_ref, box_ref, wg_ref, sc_ref, o_ref, ps_ref, *, n_rows):
    g = pl.program_id(1)
    seq = n_rows * GRID_W
    halo = POOL_HALO_TILES * POOL_TILE
    n_tiles = seq // POOL_TILE
    zeros = jnp.zeros((halo, ps_ref.shape[1]), ps_ref.dtype)
    ps_ref[pl.ds(0, halo), :] = zeros
    ps_ref[pl.ds(halo + seq, halo), :] = zeros
    ps_ref[pl.ds(halo, seq), :] = p_ref[...]

    for gi, w in enumerate(POOL_WINDOWS):
        offsets = _box_offsets(w)

        @pl.when(g == gi)
        def _(w=w, offsets=offsets):
            def tile(t, carry):
                r0 = pl.multiple_of(t * POOL_TILE, POOL_TILE)
                tot = None
                for j, off in enumerate(offsets):
                    src = ps_ref[pl.ds(halo + r0 + off * POOL_TILE, POOL_TILE), :]
                    part = jnp.dot(box_ref[j], src, preferred_element_type=F32)
                    tot = part if tot is None else tot + part
                tok = r0 + lax.broadcasted_iota(jnp.int32, (POOL_TILE, 1), 0)
                img_row = lax.shift_right_logical(tok, GRID_W.bit_length() - 1)
                img_col = jnp.bitwise_and(tok, GRID_W - 1)
                cnt = _count(img_row, n_rows, w) * _count(img_col, GRID_W, w)
                pin = p_ref[pl.ds(r0, POOL_TILE), :].astype(F32)
                diff = (tot * (1.0 / cnt.astype(F32)) - pin).astype(BF16)
                y = jnp.dot(diff, wg_ref[...], preferred_element_type=F32) * sc_ref[...]
                o_ref[pl.ds(r0, POOL_TILE), :] = y.astype(o_ref.dtype)
                return carry

            lax.fori_loop(0, n_tiles, tile, 0, unroll=4)


def _pool_lat_call(proj_l, box, wpg, pscale, layer, batch, seq, d_model):
    d_pool = d_model // 2
    grp = d_pool // len(POOL_WINDOWS)
    col0 = (d_model // 2 * 2 + 2 * d_model) // grp
    halo = POOL_HALO_TILES * POOL_TILE
    return pl.pallas_call(
        functools.partial(_pool_lat_kernel, n_rows=seq // GRID_W),
        out_shape=jax.ShapeDtypeStruct((batch * seq, d_pool), BF16),
        grid=(batch, len(POOL_WINDOWS)),
        in_specs=[pl.BlockSpec((seq, grp), lambda b, g: (b, col0 + g)),
                  pl.BlockSpec((None, box.shape[1], POOL_TILE, POOL_TILE), lambda b, g: (g, 0, 0, 0)),
                  pl.BlockSpec((None, None, grp, grp), lambda b, g: (layer, g, 0, 0)),
                  pl.BlockSpec((None, 1, grp), lambda b, g: (layer, 0, g))],
        out_specs=pl.BlockSpec((seq, grp), lambda b, g: (b, g)),
        scratch_shapes=[pltpu.VMEM((seq + 2 * halo, grp), BF16)],
        compiler_params=_cparams(2),
        name="pool_lat",
    )(proj_l, box, wpg, pscale)


def _pool_ctx_kernel(p_ref, band_ref, icnt_ref, wg_ref, sc_ref, o_ref):
    tot = jnp.dot(band_ref[...], p_ref[...], preferred_element_type=F32)
    diff = (tot * icnt_ref[...] - p_ref[...].astype(F32)).astype(BF16)
    y = jnp.dot(diff, wg_ref[...], preferred_element_type=F32) * sc_ref[...]
    o_ref[...] = y.astype(o_ref.dtype)


def _pool_ctx_call(proj_c, band, icnt, wpg, pscale, layer, batch, ctx_len, d_model):
    d_pool = d_model // 2
    grp = d_pool // len(POOL_WINDOWS)
    col0 = (d_model // 2 * 2 + 2 * d_model) // grp
    return pl.pallas_call(
        _pool_ctx_kernel,
        out_shape=jax.ShapeDtypeStruct((batch * ctx_len, d_pool), BF16),
        grid=(batch, len(POOL_WINDOWS)),
        in_specs=[pl.BlockSpec((ctx_len, grp), lambda b, g: (b, col0 + g)),
                  pl.BlockSpec((None, ctx_len, ctx_len), lambda b, g: (g, 0, 0)),
                  pl.BlockSpec((None, ctx_len, 1), lambda b, g: (g, 0, 0)),
                  pl.BlockSpec((None, None, grp, grp), lambda b, g: (layer, g, 0, 0)),
                  pl.BlockSpec((None, 1, grp), lambda b, g: (layer, 0, g))],
        out_specs=pl.BlockSpec((ctx_len, grp), lambda b, g: (b, g)),
        compiler_params=_cparams(2),
        name="pool_ctx",
    )(proj_c, band, icnt, wpg, pscale)


def _merge_kernel(ga_ref, pa_ref, bgg_ref, bgp_ref, wg_ref, wp_ref, o_ref, wgb_ref, wpb_ref):
    first = pl.program_id(1) == 0

    def merge(wg, wp):
        yg = jnp.dot(ga_ref[...], wg, preferred_element_type=F32)
        yp = jnp.dot(pa_ref[...], wp, preferred_element_type=F32)
        y = (jax.nn.sigmoid(bgg_ref[...].astype(F32)) * yg
             + jax.nn.sigmoid(bgp_ref[...].astype(F32)) * yp)
        o_ref[...] = y.astype(o_ref.dtype)

    @pl.when(first)
    def _():
        wg = wg_ref[...].astype(BF16)
        wp = wp_ref[...].astype(BF16)
        wgb_ref[...] = wg
        wpb_ref[...] = wp
        merge(wg, wp)

    @pl.when(jnp.logical_not(first))
    def _():
        merge(wgb_ref[...], wpb_ref[...])


def _merge_call(gla_act, pool_act, proj, w_gla_out, w_pool_out, layer, tm, tn, n_rows=None):
    m = gla_act.shape[0] if n_rows is None else n_rows
    d = w_gla_out.shape[-1]
    d_pool = pool_act.shape[1]
    bg0 = (proj.shape[1] - 2 * d) // tn
    return pl.pallas_call(
        _merge_kernel,
        out_shape=jax.ShapeDtypeStruct((m, d), BF16),
        grid=(d // tn, m // tm),
        in_specs=[pl.BlockSpec((tm, d), lambda j, i: (i, 0)),
                  pl.BlockSpec((tm, d_pool), lambda j, i: (i, 0)),
                  pl.BlockSpec((tm, tn), lambda j, i: (i, bg0 + j)),
                  pl.BlockSpec((tm, tn), lambda j, i: (i, bg0 + d // tn + j)),
                  pl.BlockSpec((None, d, tn), lambda j, i: (layer, 0, j)),
                  pl.BlockSpec((None, d_pool, tn), lambda j, i: (layer, 0, j))],
        out_specs=pl.BlockSpec((tm, tn), lambda j, i: (i, j)),
        scratch_shapes=[pltpu.VMEM((d, tn), BF16), pltpu.VMEM((d_pool, tn), BF16)],
        compiler_params=_cparams(2),
        name="merge",
    )(gla_act, pool_act, proj, proj, w_gla_out, w_pool_out)


def _mm_ln_kernel(*refs, row_fn, nm, nk, alpha, emit_h):
    if emit_h:
        (a_ref, w_ref, x_ref, gt_ref, lng_ref, lnb_ref, sc_ref, sh_ref, xo_ref, ho_ref, *acc) = refs
    else:
        (a_ref, w_ref, x_ref, gt_ref, lng_ref, lnb_ref, xo_ref, *acc) = refs
    i = pl.program_id(0)
    kk = pl.program_id(1)
    tq = x_ref.shape[0] // nk

    def matmul(acc_ref):
        part = jnp.dot(a_ref[...], w_ref[...], preferred_element_type=F32)
        if nk == 1:
            acc_ref[...] = part
        else:
            acc_ref[...] += part

    def epilogue(acc_ref):
        base = kk * tq
        r = row_fn(i - 1)
        gt = gt_ref[pl.ds(r, 1), :]
        lng = lng_ref[...]
        lnb = lnb_ref[...]
        if emit_h:
            sc1 = 1.0 + sc_ref[pl.ds(r, 1), :]
            sh = sh_ref[pl.ds(r, 1), :]
        for c in range(tq // LN_ROWS):
            rows = pl.ds(pl.multiple_of(base + c * LN_ROWS, LN_ROWS), LN_ROWS)
            y = alpha * x_ref[rows, :] + gt * acc_ref[rows, :]
            mu = jnp.mean(y, axis=-1, keepdims=True)
            yc = y - mu
            var = jnp.mean(yc * yc, axis=-1, keepdims=True)
            xn = yc * lax.rsqrt(var + LN_EPS) * lng + lnb
            xo_ref[rows, :] = xn
            if emit_h:
                ho_ref[rows, :] = (xn * sc1 + sh).astype(BF16)
            if nk > 1:
                acc_ref[rows, :] = jnp.zeros((LN_ROWS, acc_ref.shape[1]), F32)

    @pl.when(i == 0)
    def _():
        if nk > 1:
            @pl.when(kk == 0)
            def _():
                acc[0][...] = jnp.zeros_like(acc[0])
                acc[1][...] = jnp.zeros_like(acc[1])
        matmul(acc[0])

    for parity in (0, 1):
        @pl.when(jnp.logical_and(jnp.logical_and(i >= 1, i < nm), i % 2 == parity))
        def _(parity=parity):
            epilogue(acc[1 - parity])
            matmul(acc[parity])

    @pl.when(i == nm)
    def _():
        epilogue(acc[(nm - 1) % 2])


def _mm_ln_call(a, w, x, mods, ln_gain, ln_bias, layer, gate_chunk, next_mod, row_fn, tm, tk, alpha,
                n_rows=None):
    m = x.shape[0] if n_rows is None else n_rows
    k = a.shape[1]
    d = x.shape[1]
    nk = k // tk
    nm = m // tm
    emit_h = next_mod is not None

    def k_idx(i, kk):
        return jnp.where(i < nm, kk, nk - 1)

    prev = lambda i, kk: (jnp.maximum(i - 1, 0), 0)
    in_specs = [pl.BlockSpec((tm, tk), lambda i, kk: (jnp.minimum(i, nm - 1), k_idx(i, kk))),
                pl.BlockSpec((None, tk, d), lambda i, kk: (layer, k_idx(i, kk), 0)),
                pl.BlockSpec((tm, d), prev),
                _mod_spec(layer, gate_chunk, d),
                pl.BlockSpec((None, 1, d), lambda i, kk: (layer, 0, 0)),
                pl.BlockSpec((None, 1, d), lambda i, kk: (layer, 0, 0))]
    args = [a, w, x, mods, ln_gain, ln_bias]
    out_shape = [jax.ShapeDtypeStruct((m, d), F32)]
    out_specs = [pl.BlockSpec((tm, d), prev)]
    if emit_h:
        nl, sc_chunk, sh_chunk = next_mod
        in_specs += [_mod_spec(nl, sc_chunk, d), _mod_spec(nl, sh_chunk, d)]
        args += [mods, mods]
        out_shape.append(jax.ShapeDtypeStruct((m, d), BF16))
        out_specs.append(pl.BlockSpec((tm, d), prev))
    outs = pl.pallas_call(
        functools.partial(_mm_ln_kernel, row_fn=row_fn, nm=nm, nk=nk, alpha=alpha, emit_h=emit_h),
        out_shape=out_shape,
        grid=(nm + 1, nk),
        in_specs=in_specs,
        out_specs=out_specs,
        scratch_shapes=[pltpu.VMEM((tm, d), F32), pltpu.VMEM((tm, d), F32)],
        compiler_params=_cparams(2),
        name="mm_ln",
    )(*args)
    return (outs[0], outs[1]) if emit_h else (outs[0], None)


def _ffn1_kernel(*refs, nm, with_ctx):
    if with_ctx:
        h_ref, hc_ref, wg_ref, wu_ref, o_ref, oc_ref, wb_ref = refs
    else:
        h_ref, wg_ref, wu_ref, o_ref, wb_ref = refs
    i = pl.program_id(1)
    tn = o_ref.shape[1]

    def act(src_ref, dst_ref, w):
        y = jnp.dot(src_ref[...], w, preferred_element_type=F32)
        dst_ref[...] = (_silu(y[:, :tn]) * y[:, tn:]).astype(dst_ref.dtype)

    @pl.when(i == 0)
    def _():
        w = jnp.concatenate([wg_ref[...].astype(BF16), wu_ref[...].astype(BF16)], axis=1)
        wb_ref[...] = w
        act(h_ref, o_ref, w)

    @pl.when(jnp.logical_and(i > 0, i < nm))
    def _():
        act(h_ref, o_ref, wb_ref[...])

    if with_ctx:
        @pl.when(i == nm)
        def _():
            act(hc_ref, oc_ref, wb_ref[...])


def _ffn1_call(h, h_ctx, w_ffn_in, layer, tm, tn):
    m, d = h.shape
    nm = m // tm
    d_ff = w_ffn_in.shape[-1] // 2
    nj = d_ff // tn
    with_ctx = h_ctx is not None
    lat_row = lambda i: jnp.minimum(i, nm - 1)
    in_specs = [pl.BlockSpec((tm, d), lambda j, i: (lat_row(i), 0))]
    out_shape = [jax.ShapeDtypeStruct((m, d_ff), BF16)]
    out_specs = [pl.BlockSpec((tm, tn), lambda j, i: (lat_row(i), j))]
    args = [h]
    if with_ctx:
        mc = h_ctx.shape[0]
        in_specs.append(pl.BlockSpec((mc, d), lambda j, i: (0, 0)))
        out_shape.append(jax.ShapeDtypeStruct((mc, d_ff), BF16))
        out_specs.append(pl.BlockSpec((mc, tn), lambda j, i: (0, j)))
        args.append(h_ctx)
    in_specs += [pl.BlockSpec((None, d, tn), lambda j, i: (layer, 0, j)),
                 pl.BlockSpec((None, d, tn), lambda j, i: (layer, 0, nj + j))]
    outs = pl.pallas_call(
        functools.partial(_ffn1_kernel, nm=nm, with_ctx=with_ctx),
        out_shape=out_shape,
        grid=(nj, nm + 1 if with_ctx else nm),
        in_specs=in_specs,
        out_specs=out_specs,
        scratch_shapes=[pltpu.VMEM((d, 2 * tn), BF16)],
        compiler_params=_cparams(2),
        name="ffn1",
    )(*args, w_ffn_in, w_ffn_in)
    return (outs[0], outs[1]) if with_ctx else (outs[0], None)


def kernel(x, c, ctx, c_ctx, w_ada, b_ada, w_in, w_decay_up, b_decay_up, gla_norm_gain, w_pool_group, pool_scale, w_gla_out, w_pool_out, w_out, ln_mix_gain, ln_mix_bias, w_ffn_in, w_ffn_out, ln_ffn_gain, ln_ffn_bias):
    batch, seq, d = x.shape
    ctx_len = ctx.shape[1]
    depth = w_ada.shape[0]
    d_key = d // 2
    d_pool = d // 2
    d_ff = w_ffn_out.shape[1]
    alpha = (2.0 * depth) ** 0.25
    m_lat = batch * seq
    m_ctx = batch * ctx_len
    assert seq % (2 * GLA_BLOCK) == 0 and seq % LAT_TM == 0
    assert ctx_len % (GLA_GROUP * CHUNK) == 0 and GLA_BLOCK % (GLA_GROUP * CHUNK) == 0
    assert batch + 1 <= 8

    x_l = x.reshape(m_lat, d)
    x_c = ctx.reshape(m_ctx, d)
    cond = jnp.zeros((8, d), F32).at[:batch].set(c).at[batch].set(c_ctx)
    w_in_t = jnp.swapaxes(w_in, 1, 2)
    wup = jnp.zeros((depth, 2, LANES, d_key), F32)
    wup = wup.at[:, 0, :GATE_RANK].set(w_decay_up[:, 0]).at[:, 1, GATE_RANK:2 * GATE_RANK].set(w_decay_up[:, 1])
    wup = wup.astype(BF16)
    bup = b_decay_up.reshape(depth, 2, 1, d_key)
    gain = gla_norm_gain.reshape(depth, 1, d)
    wpg = w_pool_group.astype(BF16)
    pscale = pool_scale.reshape(depth, 1, d_pool)
    w_out_b = w_out.astype(BF16)
    w_ffn_out_b = w_ffn_out.astype(BF16)
    lnm_g = ln_mix_gain.reshape(depth, 1, d)
    lnm_b = ln_mix_bias.reshape(depth, 1, d)
    lnf_g = ln_ffn_gain.reshape(depth, 1, d)
    lnf_b = ln_ffn_bias.reshape(depth, 1, d)
    n_off = max(len(_box_offsets(w)) for w in POOL_WINDOWS)
    assert max(abs(o) for w in POOL_WINDOWS for o in _box_offsets(w)) <= POOL_HALO_TILES
    box_np = np.zeros((len(POOL_WINDOWS), n_off, POOL_TILE, POOL_TILE), np.float32)
    for gi, w in enumerate(POOL_WINDOWS):
        for j, off in enumerate(_box_offsets(w)):
            box_np[gi, j] = _box_matrix(w, off)
    box_lat = jnp.asarray(box_np, BF16)
    band_ctx = jnp.asarray(np.stack([_band_matrix(ctx_len, w) for w in POOL_WINDOWS]), BF16)
    t = np.arange(ctx_len)
    icnt_ctx = jnp.asarray(np.stack([
        1.0 / (np.minimum(t + _win(w)[1] + 1, ctx_len) - np.maximum(t - _win(w)[0], 0))
        for w in POOL_WINDOWS]).astype(np.float32)[:, :, None])

    mods = _mods_call(cond, w_ada, b_ada)

    lat_row_1024 = lambda i: i // (seq // LAT_TM)
    lat_row_512 = lambda i: i // (seq // LN_TM)
    ctx_row = lambda i: batch

    h_l = _modulate_call(x_l, mods, 0, lat_row_1024, LAT_TM)
    h_c = _modulate_call(x_c, mods, 0, ctx_row, m_ctx)

    ffn_tk = d_ff // 4
    for layer in range(depth):
        ctx_out = layer < depth - 1
        proj_l, proj_c = _in_proj_call(h_l, h_c, w_in_t, gain, layer, LAT_TM, 1024)
        alr_l = _in_proj_decay_call(h_l, w_in_t, layer, LAT_TM, d, "in_proj_decay")
        alr_c = _in_proj_decay_call(h_c, w_in_t, layer, m_ctx, d, "in_proj_decay_ctx")
        gla_l, gla_c = _gla_call(proj_l, alr_l, proj_c, alr_c, wup, bup, layer,
                                 batch, seq, ctx_len, d)
        pool_l = _pool_lat_call(proj_l, box_lat, wpg, pscale, layer, batch, seq, d)
        merged_l = _merge_call(gla_l, pool_l, proj_l, w_gla_out, w_pool_out, layer, LN_TM, 1024)
        x_l, hf_l = _mm_ln_call(merged_l, w_out_b, x_l, mods, lnm_g, lnm_b, layer, 2,
                                (layer, 4, 3), lat_row_512, LN_TM, d, alpha)
        hf_c = None
        if ctx_out:
            pool_c = _pool_ctx_call(proj_c, band_ctx, icnt_ctx, wpg, pscale, layer, batch, ctx_len, d)
            merged_c = _merge_call(gla_c, pool_c, proj_c, w_gla_out, w_pool_out, layer, m_ctx, 1024)
            x_c, hf_c = _mm_ln_call(merged_c, w_out_b, x_c, mods, lnm_g, lnm_b, layer, 2,
                                    (layer, 4, 3), ctx_row, m_ctx, d, alpha)
        act_l, act_c = _ffn1_call(hf_l, hf_c, w_ffn_in, layer, LAT_TM, 512)
        nxt = (layer + 1, 1, 0) if ctx_out else None
        x_l, h_l = _mm_ln_call(act_l, w_ffn_out_b, x_l, mods, lnf_g, lnf_b, layer, 5,
                               nxt, lat_row_512, LN_TM, ffn_tk, alpha)
        if ctx_out:
            x_c, h_c = _mm_ln_call(act_c, w_ffn_out_b, x_c, mods, lnf_g, lnf_b, layer, 5,
                                   (layer + 1, 1, 0), ctx_row, m_ctx, ffn_tk, alpha)
    return x_l.reshape(batch, seq, d)
```

```python
import functools

import numpy as np
import jax
import jax.numpy as jnp
from jax import lax
from jax.experimental import pallas as pl
from jax.experimental.pallas import tpu as pltpu

F32 = jnp.float32
BF16 = jnp.bfloat16

GRID_W = 64
N_HEADS = 4
GATE_RANK = 16
GATE_NORM = 16.0
CHUNK = 64
POOL_WINDOWS = (2, 4, 8, 16)
N_MOD = 6
LN_EPS = 1e-5
RMS_EPS = 1e-6
LOG2E = 1.4426950408889634

LANES = 128
VMEM_LIMIT = 56 * 1024 * 1024

GLA_BLOCK = 512
GLA_GROUP = 2
GLA_FINAL_ROWS = 2048
GLA_NORM_ROWS = 256
POOL_TILE = 256
POOL_HALO_TILES = 2
LAT_TM = 1024
LN_TM = 512
LN_ROWS = 16


def _cparams(n_axes, vmem=VMEM_LIMIT):
    return pltpu.CompilerParams(dimension_semantics=("arbitrary",) * n_axes,
                                vmem_limit_bytes=vmem)


def _silu(v):
    return v * jax.nn.sigmoid(v)


def _mods_kernel(cond_ref, w_ref, b_ref, o_ref):
    a = _silu(cond_ref[...]).astype(BF16)
    o_ref[...] = jnp.dot(a, w_ref[...].astype(BF16), preferred_element_type=F32) + b_ref[...]


def _mods_call(cond, w_ada, b_ada):
    depth, d, n = w_ada.shape
    tn = 1024
    return pl.pallas_call(
        _mods_kernel,
        out_shape=jax.ShapeDtypeStruct((depth, 8, n), F32),
        grid=(depth, n // tn),
        in_specs=[pl.BlockSpec((8, d), lambda l, j: (0, 0)),
                  pl.BlockSpec((None, d, tn), lambda l, j: (l, 0, j)),
                  pl.BlockSpec((None, 1, tn), lambda l, j: (l, 0, j))],
        out_specs=pl.BlockSpec((None, 8, tn), lambda l, j: (l, 0, j)),
        compiler_params=_cparams(2),
        name="mods",
    )(cond, w_ada, b_ada.reshape(depth, 1, n))


def _mod_spec(layer, chunk, d):
    return pl.BlockSpec((None, 8, d), lambda *_: (layer, 0, chunk))


def _modulate_kernel(x_ref, sc_ref, sh_ref, o_ref, *, row_fn):
    r = row_fn(pl.program_id(0))
    sc = sc_ref[pl.ds(r, 1), :]
    sh = sh_ref[pl.ds(r, 1), :]
    o_ref[...] = (x_ref[...] * (1.0 + sc) + sh).astype(BF16)


def _modulate_call(x, mods, layer, row_fn, tm):
    m, d = x.shape
    return pl.pallas_call(
        functools.partial(_modulate_kernel, row_fn=row_fn),
        out_shape=jax.ShapeDtypeStruct((m, d), BF16),
        grid=(m // tm,),
        in_specs=[pl.BlockSpec((tm, d), lambda i: (i, 0)),
                  _mod_spec(layer, 1, d), _mod_spec(layer, 0, d)],
        out_specs=pl.BlockSpec((tm, d), lambda i: (i, 0)),
        compiler_params=_cparams(1),
        name="modulate",
    )(x, mods, mods)


def _in_proj_kernel(a_ref, wt_ref, gain_ref, o_ref, wb_ref, *, n_q, g_lo, g_hi, q_scale):
    j = pl.program_id(0)
    is_q = j < n_q
    is_g = jnp.logical_and(j >= g_lo, j < g_hi)

    is_plain = jnp.logical_not(jnp.logical_or(is_q, is_g))
    first = pl.program_id(1) == 0

    for cast in (True, False):
        def mm(cast=cast):
            if cast:
                wb = wt_ref[0].astype(BF16)
                wb_ref[...] = wb
            else:
                wb = wb_ref[...]
            return lax.dot_general(a_ref[...], wb, (((1,), (1,)), ((), ())),
                                   preferred_element_type=F32)

        step = first if cast else jnp.logical_not(first)

        @pl.when(jnp.logical_and(step, is_q))
        def _(mm=mm):
            o_ref[...] = (mm() * q_scale).astype(o_ref.dtype)

        @pl.when(jnp.logical_and(step, is_g))
        def _(mm=mm):
            o_ref[...] = (_silu(mm()) * gain_ref[...]).astype(o_ref.dtype)

        @pl.when(jnp.logical_and(step, is_plain))
        def _(mm=mm):
            o_ref[...] = mm().astype(o_ref.dtype)


def _in_proj_call(a, w_in_t, gain, layer, tm, tn, name):
    m, k = a.shape
    d = gain.shape[-1]
    d_key = d // 2
    n_head = 2 * d_key + 2 * d
    n = w_in_t.shape[1] - 2 * GATE_RANK
    n_a = n_head // tn
    g_lo = (2 * d_key + d) // tn
    g_hi = g_lo + d // tn
    kern = functools.partial(_in_proj_kernel, n_q=d_key // tn, g_lo=g_lo, g_hi=g_hi,
                             q_scale=(d_key // N_HEADS) ** -0.5)

    def row_start(j):
        sub = 8
        return (j * (tn // sub) + jnp.where(j >= n_a, 2 * GATE_RANK // sub, 0)) * sub

    return pl.pallas_call(
        kern,
        out_shape=jax.ShapeDtypeStruct((m, n), BF16),
        grid=(n // tn, m // tm),
        in_specs=[pl.BlockSpec((tm, k), lambda j, i: (i, 0)),
                  pl.BlockSpec((pl.Element(1), pl.Element(tn), pl.Element(k)),
                               lambda j, i: (layer, row_start(j), 0)),
                  pl.BlockSpec((None, 1, tn),
                               lambda j, i: (layer, 0, jnp.clip(j - g_lo, 0, g_hi - g_lo - 1)))],
        out_specs=pl.BlockSpec((tm, tn), lambda j, i: (i, j)),
        scratch_shapes=[pltpu.VMEM((tn, k), BF16)],
        compiler_params=_cparams(2),
        name=name,
    )(a, w_in_t, gain)


def _in_proj_decay_kernel(a_ref, wt_ref, o_ref, *, n_valid):
    y = lax.dot_general(a_ref[...], wt_ref[...].astype(BF16), (((1,), (1,)), ((), ())),
                        preferred_element_type=F32)
    lane = lax.broadcasted_iota(jnp.int32, y.shape, 1)
    o_ref[...] = jnp.where(lane < n_valid, y, 0.0)


def _in_proj_decay_call(a, w_in_t, layer, tm, d, name):
    m, k = a.shape
    row0 = 2 * (d // 2) + 2 * d
    assert row0 % LANES == 0
    return pl.pallas_call(
        functools.partial(_in_proj_decay_kernel, n_valid=2 * GATE_RANK),
        out_shape=jax.ShapeDtypeStruct((m, LANES), F32),
        grid=(m // tm,),
        in_specs=[pl.BlockSpec((tm, k), lambda i: (i, 0)),
                  pl.BlockSpec((None, LANES, k), lambda i: (layer, row0 // LANES, 0))],
        out_specs=pl.BlockSpec((tm, LANES), lambda i: (i, 0)),
        compiler_params=_cparams(1),
        name=name,
    )(a, w_in_t)


def _gla_kernel(qf_ref, kf_ref, vf_ref, af_ref, qb_ref, kb_ref, vb_ref, ab_ref, g_ref,
                qc_ref, kc_ref, vc_ref, ac_ref, gc_ref,
                wup_ref, bup_ref,
                out_ref, outc_ref,
                state_ref, acc_ref, accc_ref, *, nb):
    s = pl.program_id(2)
    fin_rows = out_ref.shape[0]

    row = lax.broadcasted_iota(jnp.int32, (CHUNK, CHUNK), 0)
    col = lax.broadcasted_iota(jnp.int32, (CHUNK, CHUNK), 1)
    keep = (row >= col, row <= col)
    row2 = lax.broadcasted_iota(jnp.int32, (CHUNK, 2 * CHUNK), 0)
    col2 = jnp.bitwise_and(lax.broadcasted_iota(jnp.int32, (CHUNK, 2 * CHUNK), 1), CHUNK - 1)
    keep2 = (row2 >= col2, row2 <= col2)
    ref_row = (CHUNK // 2 - 1, CHUNK // 2)
    last_row = (CHUNK - 1, 0)

    def log_decay(d, ar):
        z = jnp.dot(ar[...].astype(BF16), wup_ref[d], preferred_element_type=F32) + bup_ref[d]
        log_a = ((jnp.minimum(z, 0.0) - jnp.log(1.0 + jnp.exp2(jnp.abs(z) * (-LOG2E))))
                 * (LOG2E / GATE_NORM))
        hi = log_a.astype(BF16)
        return hi, (log_a - hi.astype(F32)).astype(BF16)

    def group_intra(d, tri, hi, lo, qr, kr, vr, g0):
        n = GLA_GROUP
        scan = list(range(n - 1, -1, -1) if d == 1 else range(n))
        cum, last, q, k, diag = {}, {}, {}, {}, {}
        for a in range(n):
            sl = slice((g0 + a) * CHUNK, (g0 + a + 1) * CHUNK)
            cum[a] = jnp.dot(tri, jnp.concatenate([hi[sl], lo[sl]], axis=0), preferred_element_type=F32)
            ref = cum[a][ref_row[d]:ref_row[d] + 1, :]
            last[a] = cum[a][last_row[d]:last_row[d] + 1, :]
            q[a] = qr[sl, :].astype(F32)
            k[a] = kr[sl, :].astype(F32)
            q_in = (q[a] * jnp.exp2(cum[a] - ref)).astype(BF16)
            k_in = (k[a] * jnp.exp2(ref - cum[a])).astype(BF16)
            s = lax.dot_general(q_in, k_in, (((1,), (1,)), ((), ())), preferred_element_type=F32)
            diag[a] = jnp.where(keep[d], s, 0.0)
        before = [None] * n
        after = [None] * n
        run = None
        for i in range(n):
            before[i] = run
            run = last[scan[i]] if run is None else run + last[scan[i]]
        total = run
        run = None
        for i in range(n - 1, -1, -1):
            after[i] = run
            run = last[scan[i]] if run is None else run + last[scan[i]]

        def scaled(x, log_factor):
            return (x if log_factor is None else x * jnp.exp2(log_factor)).astype(BF16)

        q_dec = {a: q[a] * jnp.exp2(cum[a]) for a in range(n)}
        k_dec = {a: k[a] * jnp.exp2(last[a] - cum[a]) for a in range(n)}
        k_plain = {a: k_dec[a].astype(BF16) for a in range(n)}
        zero_blk = jnp.zeros((CHUNK, CHUNK), F32)
        rows = [None] * n
        for i in range(n):
            ai = scan[i]
            blocks = {ai: diag[ai]}
            between = None
            for j in range(i - 1, -1, -1):
                aj = scan[j]
                blocks[aj] = lax.dot_general(scaled(q_dec[ai], between), k_plain[aj],
                                             (((1,), (1,)), ((), ())), preferred_element_type=F32)
                between = last[aj] if between is None else between + last[aj]
            rows[ai] = jnp.concatenate([blocks.get(a, zero_blk) for a in range(n)], axis=1).astype(BF16)
        p_mat = jnp.concatenate(rows, axis=0)
        q_state = jnp.concatenate([scaled(q_dec[a], before[scan.index(a)]) for a in range(n)], axis=0)
        k_state = jnp.concatenate([scaled(k_dec[a], after[scan.index(a)]) for a in range(n)], axis=0)
        v = vr[g0 * CHUNK:(g0 + n) * CHUNK, :]
        o_intra = jnp.dot(p_mat, v, preferred_element_type=F32)
        upd = lax.dot_general(v, k_state, (((0,), (0,)), ((), ())), preferred_element_type=F32)
        return o_intra, q_state, upd, jnp.exp2(total)

    def scan_pair(fwd, bwd, acc, off_f, off_b):
        streams = []
        for d, (qr, kr, vr, ar), off in ((0, fwd, off_f), (1, bwd, off_b)):
            n_groups = qr.shape[0] // (GLA_GROUP * CHUNK)
            order = list(range(n_groups - 1, -1, -1) if d == 1 else range(n_groups))
            hi, lo = log_decay(d, ar)
            tri = jnp.where(keep2[d], 1.0, 0.0).astype(BF16)
            streams.append((d, tri, qr, kr, vr, hi, lo, order, off))
        n_steps = len(streams[0][7])
        pre = {}
        st = [state_ref[0], state_ref[1]]
        rows = GLA_GROUP * CHUNK
        for i in range(n_steps):
            for d, tri, qr, kr, vr, hi, lo, order, _ in streams:
                pre[d, i] = group_intra(d, tri, hi, lo, qr, kr, vr, order[i] * GLA_GROUP)
            for d, _, _, _, _, _, _, order, off in streams:
                o_intra, q_state, upd, decay = pre[d, i]
                o = o_intra + lax.dot_general(q_state, st[d].astype(BF16), (((1,), (1,)), ((), ())),
                                              preferred_element_type=F32)
                st[d] = st[d] * decay + upd
                ra = pl.multiple_of(off + order[i] * rows, rows)
                acc[pl.ds(ra, rows), :] += o
        state_ref[0] = st[0]
        state_ref[1] = st[1]

    def finalize(acc, base, gr, outr):
        rows = min(GLA_NORM_ROWS, outr.shape[0])

        def body(c, carry):
            r0 = pl.multiple_of(c * rows, rows)
            o = acc[pl.ds(pl.multiple_of(base + r0, rows), rows), :]
            ms = jnp.mean(o * o, axis=-1, keepdims=True)
            y = o * lax.rsqrt(ms + RMS_EPS) * gr[pl.ds(r0, rows), :].astype(F32)
            outr[pl.ds(r0, rows), :] = y.astype(outr.dtype)
            return carry

        lax.fori_loop(0, outr.shape[0] // rows, body, 0)

    ctx = (qc_ref, kc_ref, vc_ref, ac_ref)

    @pl.when(s == 0)
    def _():
        state_ref[...] = jnp.zeros_like(state_ref)
        acc_ref[...] = jnp.zeros_like(acc_ref)
        accc_ref[...] = jnp.zeros_like(accc_ref)
        scan_pair(ctx, ctx, accc_ref, 0, 0)

    @pl.when(s < nb)
    def _():
        scan_pair((qf_ref, kf_ref, vf_ref, af_ref), (qb_ref, kb_ref, vb_ref, ab_ref), acc_ref,
                  s * GLA_BLOCK, (nb - 1 - s) * GLA_BLOCK)

    @pl.when(s == nb)
    def _():
        finalize(accc_ref, 0, gc_ref, outc_ref)

    @pl.when(s >= nb)
    def _():
        finalize(acc_ref, (s - nb) * fin_rows, g_ref, out_ref)


def _gla_call(proj_l, alr_l, proj_c, alr_c, wup, bup, layer, batch, seq, ctx_len, d_model):
    d_key = d_model // 2
    head_k = d_key // N_HEADS
    head_v = d_model // N_HEADS
    nb = seq // GLA_BLOCK
    fin_rows = min(GLA_FINAL_ROWS, seq)
    nf = seq // fin_rows
    kq = d_key // head_k
    kv = (2 * d_key) // head_v
    kg = kv + d_model // head_v

    def blk(s, rev):
        return jnp.maximum(nb - 1 - s, 0) if rev else jnp.minimum(s, nb - 1)

    def lat(width, col0, rev):
        return pl.BlockSpec((GLA_BLOCK, width), lambda b, h, s: (b * nb + blk(s, rev), col0 + h))

    def lat_a(rev):
        return pl.BlockSpec((GLA_BLOCK, LANES), lambda b, h, s: (b * nb + blk(s, rev), 0))

    def fin(col0):
        return pl.BlockSpec((fin_rows, head_v),
                            lambda b, h, s: (b * nf + jnp.clip(s - nb, 0, nf - 1), col0 + h))

    def cx(width, col0):
        return pl.BlockSpec((ctx_len, width), lambda b, h, s: (b, col0 + h))

    in_specs = []
    for rev in (False, True):
        in_specs += [lat(head_k, 0, rev), lat(head_k, kq, rev), lat(head_v, kv, rev), lat_a(rev)]
    in_specs += [fin(kg),
                 cx(head_k, 0), cx(head_k, kq), cx(head_v, kv),
                 pl.BlockSpec((ctx_len, LANES), lambda b, h, s: (b, 0)),
                 cx(head_v, kg),
                 pl.BlockSpec((None, 2, LANES, head_k), lambda b, h, s: (layer, 0, 0, h)),
                 pl.BlockSpec((None, 2, 1, head_k), lambda b, h, s: (layer, 0, 0, h))]
    out_specs = [fin(0), cx(head_v, 0)]
    return pl.pallas_call(
        functools.partial(_gla_kernel, nb=nb),
        out_shape=(jax.ShapeDtypeStruct((batch * seq, d_model), BF16),
                   jax.ShapeDtypeStruct((batch * ctx_len, d_model), BF16)),
        grid=(batch, N_HEADS, nb + nf),
        in_specs=in_specs,
        out_specs=out_specs,
        scratch_shapes=[pltpu.VMEM((2, head_v, head_k), F32),
                        pltpu.VMEM((seq, head_v), F32),
                        pltpu.VMEM((ctx_len, head_v), F32)],
        compiler_params=_cparams(3),
        name="gla",
    )(proj_l, proj_l, proj_l, alr_l, proj_l, proj_l, proj_l, alr_l, proj_l,
      proj_c, proj_c, proj_c, alr_c, proj_c, wup, bup)


def _win(w):
    lo = w // 2
    return lo, w - lo - 1


def _band_matrix(n_tokens, w):
    lo, hi = _win(w)
    t = np.arange(n_tokens)
    off = t[None, :] - t[:, None]
    return ((off >= -lo) & (off <= hi)).astype(np.float32)


def _box_offsets(w):
    lo, hi = _win(w)
    rpt = POOL_TILE // GRID_W
    return list(range(-((lo + rpt - 1) // rpt), (hi + rpt - 1) // rpt + 1))


def _box_matrix(w, tile_offset):
    lo, hi = _win(w)
    rpt = POOL_TILE // GRID_W
    t = np.arange(POOL_TILE)
    a, c = t // GRID_W, t % GRID_W
    drow = rpt * tile_offset + a[None, :] - a[:, None]
    dcol = c[None, :] - c[:, None]
    return ((drow >= -lo) & (drow <= hi) & (dcol >= -lo) & (dcol <= hi)).astype(np.float32)


def _count(idx, n, w):
    lo, hi = _win(w)
    return jnp.minimum(idx + hi + 1, n) - jnp.maximum(idx - lo, 0)


def _pool_lat_kernel(p_ref, box_ref, wg_ref, sc_ref, o_ref, ps_ref, *, n_rows):
    g = pl.program_id(1)
    seq = n_rows * GRID_W
    halo = POOL_HALO_TILES * POOL_TILE
    n_tiles = seq // POOL_TILE
    zeros = jnp.zeros((halo, ps_ref.shape[1]), ps_ref.dtype)
    ps_ref[pl.ds(0, halo), :] = zeros
    ps_ref[pl.ds(halo + seq, halo), :] = zeros
    ps_ref[pl.ds(halo, seq), :] = p_ref[...]

    for gi, w in enumerate(POOL_WINDOWS):
        offsets = _box_offsets(w)

        @pl.when(g == gi)
        def _(w=w, offsets=offsets):
            def tile(t, carry):
                r0 = pl.multiple_of(t * POOL_TILE, POOL_TILE)
                tot = None
                for j, off in enumerate(offsets):
                    src = ps_ref[pl.ds(halo + r0 + off * POOL_TILE, POOL_TILE), :]
                    part = jnp.dot(box_ref[j], src, preferred_element_type=F32)
                    tot = part if tot is None else tot + part
                tok = r0 + lax.broadcasted_iota(jnp.int32, (POOL_TILE, 1), 0)
                img_row = lax.shift_right_logical(tok, GRID_W.bit_length() - 1)
                img_col = jnp.bitwise_and(tok, GRID_W - 1)
                cnt = _count(img_row, n_rows, w) * _count(img_col, GRID_W, w)
                pin = p_ref[pl.ds(r0, POOL_TILE), :].astype(F32)
                diff = (tot * (1.0 / cnt.astype(F32)) - pin).astype(BF16)
                y = jnp.dot(diff, wg_ref[...], preferred_element_type=F32) * sc_ref[...]
                o_ref[pl.ds(r0, POOL_TILE), :] = y.astype(o_ref.dtype)
                return carry

            lax.fori_loop(0, n_tiles, tile, 0, unroll=4)


def _pool_lat_call(proj_l, box, wpg, pscale, layer, batch, seq, d_model):
    d_pool = d_model // 2
    grp = d_pool // len(POOL_WINDOWS)
    col0 = (d_model // 2 * 2 + 2 * d_model) // grp
    halo = POOL_HALO_TILES * POOL_TILE
    return pl.pallas_call(
        functools.partial(_pool_lat_kernel, n_rows=seq // GRID_W),
        out_shape=jax.ShapeDtypeStruct((batch * seq, d_pool), BF16),
        grid=(batch, len(POOL_WINDOWS)),
        in_specs=[pl.BlockSpec((seq, grp), lambda b, g: (b, col0 + g)),
                  pl.BlockSpec((None, box.shape[1], POOL_TILE, POOL_TILE), lambda b, g: (g, 0, 0, 0)),
                  pl.BlockSpec((None, None, grp, grp), lambda b, g: (layer, g, 0, 0)),
                  pl.BlockSpec((None, 1, grp), lambda b, g: (layer, 0, g))],
        out_specs=pl.BlockSpec((seq, grp), lambda b, g: (b, g)),
        scratch_shapes=[pltpu.VMEM((seq + 2 * halo, grp), BF16)],
        compiler_params=_cparams(2),
        name="pool_lat",
    )(proj_l, box, wpg, pscale)


def _pool_ctx_kernel(p_ref, band_ref, icnt_ref, wg_ref, sc_ref, o_ref):
    tot = jnp.dot(band_ref[...], p_ref[...], preferred_element_type=F32)
    diff = (tot * icnt_ref[...] - p_ref[...].astype(F32)).astype(BF16)
    y = jnp.dot(diff, wg_ref[...], preferred_element_type=F32) * sc_ref[...]
    o_ref[...] = y.astype(o_ref.dtype)


def _pool_ctx_call(proj_c, band, icnt, wpg, pscale, layer, batch, ctx_len, d_model):
    d_pool = d_model // 2
    grp = d_pool // len(POOL_WINDOWS)
    col0 = (d_model // 2 * 2 + 2 * d_model) // grp
    return pl.pallas_call(
        _pool_ctx_kernel,
        out_shape=jax.ShapeDtypeStruct((batch * ctx_len, d_pool), BF16),
        grid=(batch, len(POOL_WINDOWS)),
        in_specs=[pl.BlockSpec((ctx_len, grp), lambda b, g: (b, col0 + g)),
                  pl.BlockSpec((None, ctx_len, ctx_len), lambda b, g: (g, 0, 0)),
                  pl.BlockSpec((None, ctx_len, 1), lambda b, g: (g, 0, 0)),
                  pl.BlockSpec((None, None, grp, grp), lambda b, g: (layer, g, 0, 0)),
                  pl.BlockSpec((None, 1, grp), lambda b, g: (layer, 0, g))],
        out_specs=pl.BlockSpec((ctx_len, grp), lambda b, g: (b, g)),
        compiler_params=_cparams(2),
        name="pool_ctx",
    )(proj_c, band, icnt, wpg, pscale)


def _merge_kernel(ga_ref, pa_ref, bgg_ref, bgp_ref, wg_ref, wp_ref, o_ref, wgb_ref, wpb_ref):
    first = pl.program_id(1) == 0

    def merge(wg, wp):
        yg = jnp.dot(ga_ref[...], wg, preferred_element_type=F32)
        yp = jnp.dot(pa_ref[...], wp, preferred_element_type=F32)
        y = (jax.nn.sigmoid(bgg_ref[...].astype(F32)) * yg
             + jax.nn.sigmoid(bgp_ref[...].astype(F32)) * yp)
        o_ref[...] = y.astype(o_ref.dtype)

    @pl.when(first)
    def _():
        wg = wg_ref[...].astype(BF16)
        wp = wp_ref[...].astype(BF16)
        wgb_ref[...] = wg
        wpb_ref[...] = wp
        merge(wg, wp)

    @pl.when(jnp.logical_not(first))
    def _():
        merge(wgb_ref[...], wpb_ref[...])


def _merge_call(gla_act, pool_act, proj, w_gla_out, w_pool_out, layer, tm, tn, n_rows=None):
    m = gla_act.shape[0] if n_rows is None else n_rows
    d = w_gla_out.shape[-1]
    d_pool = pool_act.shape[1]
    bg0 = (proj.shape[1] - 2 * d) // tn
    return pl.pallas_call(
        _merge_kernel,
        out_shape=jax.ShapeDtypeStruct((m, d), BF16),
        grid=(d // tn, m // tm),
        in_specs=[pl.BlockSpec((tm, d), lambda j, i: (i, 0)),
                  pl.BlockSpec((tm, d_pool), lambda j, i: (i, 0)),
                  pl.BlockSpec((tm, tn), lambda j, i: (i, bg0 + j)),
                  pl.BlockSpec((tm, tn), lambda j, i: (i, bg0 + d // tn + j)),
                  pl.BlockSpec((None, d, tn), lambda j, i: (layer, 0, j)),
                  pl.BlockSpec((None, d_pool, tn), lambda j, i: (layer, 0, j))],
        out_specs=pl.BlockSpec((tm, tn), lambda j, i: (i, j)),
        scratch_shapes=[pltpu.VMEM((d, tn), BF16), pltpu.VMEM((d_pool, tn), BF16)],
        compiler_params=_cparams(2),
        name="merge",
    )(gla_act, pool_act, proj, proj, w_gla_out, w_pool_out)


def _mm_ln_kernel(*refs, row_fn, nm, nk, alpha, emit_h):
    if emit_h:
        (a_ref, w_ref, x_ref, gt_ref, lng_ref, lnb_ref, sc_ref, sh_ref, xo_ref, ho_ref, *acc) = refs
    else:
        (a_ref, w_ref, x_ref, gt_ref, lng_ref, lnb_ref, xo_ref, *acc) = refs
    i = pl.program_id(0)
    kk = pl.program_id(1)
    tq = x_ref.shape[0] // nk

    def matmul(acc_ref):
        part = jnp.dot(a_ref[...], w_ref[...], preferred_element_type=F32)
        if nk == 1:
            acc_ref[...] = part
        else:
            acc_ref[...] += part

    def epilogue(acc_ref):
        base = kk * tq
        r = row_fn(i - 1)
        gt = gt_ref[pl.ds(r, 1), :]
        lng = lng_ref[...]
        lnb = lnb_ref[...]
        if emit_h:
            sc1 = 1.0 + sc_ref[pl.ds(r, 1), :]
            sh = sh_ref[pl.ds(r, 1), :]
        for c in range(tq // LN_ROWS):
            rows = pl.ds(pl.multiple_of(base + c * LN_ROWS, LN_ROWS), LN_ROWS)
            y = alpha * x_ref[rows, :] + gt * acc_ref[rows, :]
            mu = jnp.mean(y, axis=-1, keepdims=True)
            yc = y - mu
            var = jnp.mean(yc * yc, axis=-1, keepdims=True)
            xn = yc * lax.rsqrt(var + LN_EPS) * lng + lnb
            xo_ref[rows, :] = xn
            if emit_h:
                ho_ref[rows, :] = (xn * sc1 + sh).astype(BF16)
            if nk > 1:
                acc_ref[rows, :] = jnp.zeros((LN_ROWS, acc_ref.shape[1]), F32)

    @pl.when(i == 0)
    def _():
        if nk > 1:
            @pl.when(kk == 0)
            def _():
                acc[0][...] = jnp.zeros_like(acc[0])
                acc[1][...] = jnp.zeros_like(acc[1])
        matmul(acc[0])

    for parity in (0, 1):
        @pl.when(jnp.logical_and(jnp.logical_and(i >= 1, i < nm), i % 2 == parity))
        def _(parity=parity):
            epilogue(acc[1 - parity])
            matmul(acc[parity])

    @pl.when(i == nm)
    def _():
        epilogue(acc[(nm - 1) % 2])


def _mm_ln_call(a, w, x, mods, ln_gain, ln_bias, layer, gate_chunk, next_mod, row_fn, tm, tk, alpha,
                n_rows=None):
    m = x.shape[0] if n_rows is None else n_rows
    k = a.shape[1]
    d = x.shape[1]
    nk = k // tk
    nm = m // tm
    emit_h = next_mod is not None

    def k_idx(i, kk):
        return jnp.where(i < nm, kk, nk - 1)

    prev = lambda i, kk: (jnp.maximum(i - 1, 0), 0)
    in_specs = [pl.BlockSpec((tm, tk), lambda i, kk: (jnp.minimum(i, nm - 1), k_idx(i, kk))),
                pl.BlockSpec((None, tk, d), lambda i, kk: (layer, k_idx(i, kk), 0)),
                pl.BlockSpec((tm, d), prev),
                _mod_spec(layer, gate_chunk, d),
                pl.BlockSpec((None, 1, d), lambda i, kk: (layer, 0, 0)),
                pl.BlockSpec((None, 1, d), lambda i, kk: (layer, 0, 0))]
    args = [a, w, x, mods, ln_gain, ln_bias]
    out_shape = [jax.ShapeDtypeStruct((m, d), F32)]
    out_specs = [pl.BlockSpec((tm, d), prev)]
    if emit_h:
        nl, sc_chunk, sh_chunk = next_mod
        in_specs += [_mod_spec(nl, sc_chunk, d), _mod_spec(nl, sh_chunk, d)]
        args += [mods, mods]
        out_shape.append(jax.ShapeDtypeStruct((m, d), BF16))
        out_specs.append(pl.BlockSpec((tm, d), prev))
    outs = pl.pallas_call(
        functools.partial(_mm_ln_kernel, row_fn=row_fn, nm=nm, nk=nk, alpha=alpha, emit_h=emit_h),
        out_shape=out_shape,
        grid=(nm + 1, nk),
        in_specs=in_specs,
        out_specs=out_specs,
        scratch_shapes=[pltpu.VMEM((tm, d), F32), pltpu.VMEM((tm, d), F32)],
        compiler_params=_cparams(2),
        name="mm_ln",
    )(*args)
    return (outs[0], outs[1]) if emit_h else (outs[0], None)


def _ffn1_kernel(h_ref, wg_ref, wu_ref, o_ref, wgb_ref, wub_ref):
    first = pl.program_id(1) == 0

    def act(wg, wu):
        h = h_ref[...]
        gate = jnp.dot(h, wg, preferred_element_type=F32)
        up = jnp.dot(h, wu, preferred_element_type=F32)
        o_ref[...] = (_silu(gate) * up).astype(o_ref.dtype)

    @pl.when(first)
    def _():
        wg = wg_ref[...].astype(BF16)
        wu = wu_ref[...].astype(BF16)
        wgb_ref[...] = wg
        wub_ref[...] = wu
        act(wg, wu)

    @pl.when(jnp.logical_not(first))
    def _():
        act(wgb_ref[...], wub_ref[...])


def _ffn1_call(h, w_ffn_in, layer, tm, tn, n_rows=None):
    m = h.shape[0] if n_rows is None else n_rows
    d = h.shape[1]
    d_ff = w_ffn_in.shape[-1] // 2
    nj = d_ff // tn
    return pl.pallas_call(
        _ffn1_kernel,
        out_shape=jax.ShapeDtypeStruct((m, d_ff), BF16),
        grid=(nj, m // tm),
        in_specs=[pl.BlockSpec((tm, d), lambda j, i: (i, 0)),
                  pl.BlockSpec((None, d, tn), lambda j, i: (layer, 0, j)),
                  pl.BlockSpec((None, d, tn), lambda j, i: (layer, 0, nj + j))],
        out_specs=pl.BlockSpec((tm, tn), lambda j, i: (i, j)),
        scratch_shapes=[pltpu.VMEM((d, tn), BF16), pltpu.VMEM((d, tn), BF16)],
        compiler_params=_cparams(2),
        name="ffn1",
    )(h, w_ffn_in, w_ffn_in)


def kernel(x, c, ctx, c_ctx, w_ada, b_ada, w_in, w_decay_up, b_decay_up, gla_norm_gain, w_pool_group, pool_scale, w_gla_out, w_pool_out, w_out, ln_mix_gain, ln_mix_bias, w_ffn_in, w_ffn_out, ln_ffn_gain, ln_ffn_bias):
    batch, seq, d = x.shape
    ctx_len = ctx.shape[1]
    depth = w_ada.shape[0]
    d_key = d // 2
    d_pool = d // 2
    d_ff = w_ffn_out.shape[1]
    alpha = (2.0 * depth) ** 0.25
    m_lat = batch * seq
    m_ctx = batch * ctx_len
    assert seq % (2 * GLA_BLOCK) == 0 and seq % LAT_TM == 0
    assert ctx_len % (GLA_GROUP * CHUNK) == 0 and GLA_BLOCK % (GLA_GROUP * CHUNK) == 0
    assert batch + 1 <= 8

    x_l = x.reshape(m_lat, d)
    x_c = ctx.reshape(m_ctx, d)
    cond = jnp.zeros((8, d), F32).at[:batch].set(c).at[batch].set(c_ctx)
    w_in_t = jnp.swapaxes(w_in, 1, 2)
    wup = jnp.zeros((depth, 2, LANES, d_key), F32)
    wup = wup.at[:, 0, :GATE_RANK].set(w_decay_up[:, 0]).at[:, 1, GATE_RANK:2 * GATE_RANK].set(w_decay_up[:, 1])
    wup = wup.astype(BF16)
    bup = b_decay_up.reshape(depth, 2, 1, d_key)
    gain = gla_norm_gain.reshape(depth, 1, d)
    wpg = w_pool_group.astype(BF16)
    pscale = pool_scale.reshape(depth, 1, d_pool)
    w_out_b = w_out.astype(BF16)
    w_ffn_out_b = w_ffn_out.astype(BF16)
    lnm_g = ln_mix_gain.reshape(depth, 1, d)
    lnm_b = ln_mix_bias.reshape(depth, 1, d)
    lnf_g = ln_ffn_gain.reshape(depth, 1, d)
    lnf_b = ln_ffn_bias.reshape(depth, 1, d)
    n_off = max(len(_box_offsets(w)) for w in POOL_WINDOWS)
    assert max(abs(o) for w in POOL_WINDOWS for o in _box_offsets(w)) <= POOL_HALO_TILES
    box_np = np.zeros((len(POOL_WINDOWS), n_off, POOL_TILE, POOL_TILE), np.float32)
    for gi, w in enumerate(POOL_WINDOWS):
        for j, off in enumerate(_box_offsets(w)):
            box_np[gi, j] = _box_matrix(w, off)
    box_lat = jnp.asarray(box_np, BF16)
    band_ctx = jnp.asarray(np.stack([_band_matrix(ctx_len, w) for w in POOL_WINDOWS]), BF16)
    t = np.arange(ctx_len)
    icnt_ctx = jnp.asarray(np.stack([
        1.0 / (np.minimum(t + _win(w)[1] + 1, ctx_len) - np.maximum(t - _win(w)[0], 0))
        for w in POOL_WINDOWS]).astype(np.float32)[:, :, None])

    mods = _mods_call(cond, w_ada, b_ada)

    lat_row_1024 = lambda i: i // (seq // LAT_TM)
    lat_row_512 = lambda i: i // (seq // LN_TM)
    ctx_row = lambda i: batch

    h_l = _modulate_call(x_l, mods, 0, lat_row_1024, LAT_TM)
    h_c = _modulate_call(x_c, mods, 0, ctx_row, m_ctx)

    ffn_tk = d_ff // 4
    for layer in range(depth):
        ctx_out = layer < depth - 1
        proj_l = _in_proj_call(h_l, w_in_t, gain, layer, LAT_TM, 1024, "in_proj")
        alr_l = _in_proj_decay_call(h_l, w_in_t, layer, LAT_TM, d, "in_proj_decay")
        proj_c = _in_proj_call(h_c, w_in_t, gain, layer, m_ctx, 1024, "in_proj_ctx")
        alr_c = _in_proj_decay_call(h_c, w_in_t, layer, m_ctx, d, "in_proj_decay_ctx")
        gla_l, gla_c = _gla_call(proj_l, alr_l, proj_c, alr_c, wup, bup, layer,
                                 batch, seq, ctx_len, d)
        pool_l = _pool_lat_call(proj_l, box_lat, wpg, pscale, layer, batch, seq, d)
        merged_l = _merge_call(gla_l, pool_l, proj_l, w_gla_out, w_pool_out, layer, LN_TM, 1024)
        x_l, hf_l = _mm_ln_call(merged_l, w_out_b, x_l, mods, lnm_g, lnm_b, layer, 2,
                                (layer, 4, 3), lat_row_512, LN_TM, d, alpha)
        act_l = _ffn1_call(hf_l, w_ffn_in, layer, LAT_TM, 512)
        nxt = (layer + 1, 1, 0) if ctx_out else None
        x_l, h_l = _mm_ln_call(act_l, w_ffn_out_b, x_l, mods, lnf_g, lnf_b, layer, 5,
                               nxt, lat_row_512, LN_TM, ffn_tk, alpha)
        if ctx_out:
            pool_c = _pool_ctx_call(proj_c, band_ctx, icnt_ctx, wpg, pscale, layer, batch, ctx_len, d)
            merged_c = _merge_call(gla_c, pool_c, proj_c, w_gla_out, w_pool_out, layer, m_ctx, 1024)
            x_c, hf_c = _mm_ln_call(merged_c, w_out_b, x_c, mods, lnm_g, lnm_b, layer, 2,
                                    (layer, 4, 3), ctx_row, m_ctx, d, alpha)
            act_c = _ffn1_call(hf_c, w_ffn_in, layer, m_ctx, 512)
            x_c, h_c = _mm_ln_call(act_c, w_ffn_out_b, x_c, mods, lnf_g, lnf_b, layer, 5,
                                   (layer + 1, 1, 0), ctx_row, m_ctx, ffn_tk, alpha)
    return x_l.reshape(batch, seq, d)
```

```python
import functools

import numpy as np
import jax
import jax.numpy as jnp
from jax import lax
from jax.experimental import pallas as pl
from jax.experimental.pallas import tpu as pltpu

F32 = jnp.float32
BF16 = jnp.bfloat16

GRID_W = 64
N_HEADS = 4
GATE_RANK = 16
GATE_NORM = 16.0
CHUNK = 64
POOL_WINDOWS = (2, 4, 8, 16)
N_MOD = 6
LN_EPS = 1e-5
RMS_EPS = 1e-6
LOG2E = 1.4426950408889634

LANES = 128
VMEM_LIMIT = 56 * 1024 * 1024

GLA_BLOCK = 512
GLA_GROUP = 2
GLA_FINAL_ROWS = 2048
GLA_NORM_ROWS = 256
POOL_TILE = 256
POOL_HALO_TILES = 2
LAT_TM = 1024
LN_TM = 512
LN_ROWS = 16


def _cparams(n_axes, vmem=VMEM_LIMIT):
    return pltpu.CompilerParams(dimension_semantics=("arbitrary",) * n_axes,
                                vmem_limit_bytes=vmem)


def _silu(v):
    return v * jax.nn.sigmoid(v)


def _mods_kernel(cond_ref, w_ref, b_ref, o_ref):
    a = _silu(cond_ref[...]).astype(BF16)
    o_ref[...] = jnp.dot(a, w_ref[...].astype(BF16), preferred_element_type=F32) + b_ref[...]


def _mods_call(cond, w_ada, b_ada):
    depth, d, n = w_ada.shape
    tn = 1024
    return pl.pallas_call(
        _mods_kernel,
        out_shape=jax.ShapeDtypeStruct((depth, 8, n), F32),
        grid=(depth, n // tn),
        in_specs=[pl.BlockSpec((8, d), lambda l, j: (0, 0)),
                  pl.BlockSpec((None, d, tn), lambda l, j: (l, 0, j)),
                  pl.BlockSpec((None, 1, tn), lambda l, j: (l, 0, j))],
        out_specs=pl.BlockSpec((None, 8, tn), lambda l, j: (l, 0, j)),
        compiler_params=_cparams(2),
        name="mods",
    )(cond, w_ada, b_ada.reshape(depth, 1, n))


def _mod_spec(layer, chunk, d):
    return pl.BlockSpec((None, 8, d), lambda *_: (layer, 0, chunk))


def _modulate_kernel(x_ref, sc_ref, sh_ref, o_ref, *, row_fn):
    r = row_fn(pl.program_id(0))
    sc = sc_ref[pl.ds(r, 1), :]
    sh = sh_ref[pl.ds(r, 1), :]
    o_ref[...] = (x_ref[...] * (1.0 + sc) + sh).astype(BF16)


def _modulate_call(x, mods, layer, row_fn, tm):
    m, d = x.shape
    return pl.pallas_call(
        functools.partial(_modulate_kernel, row_fn=row_fn),
        out_shape=jax.ShapeDtypeStruct((m, d), BF16),
        grid=(m // tm,),
        in_specs=[pl.BlockSpec((tm, d), lambda i: (i, 0)),
                  _mod_spec(layer, 1, d), _mod_spec(layer, 0, d)],
        out_specs=pl.BlockSpec((tm, d), lambda i: (i, 0)),
        compiler_params=_cparams(1),
        name="modulate",
    )(x, mods, mods)


def _in_proj_kernel(a_ref, wt_ref, gain_ref, o_ref, wb_ref, *, n_q, g_lo, g_hi, q_scale):
    j = pl.program_id(0)
    is_q = j < n_q
    is_g = jnp.logical_and(j >= g_lo, j < g_hi)

    is_plain = jnp.logical_not(jnp.logical_or(is_q, is_g))
    first = pl.program_id(1) == 0

    for cast in (True, False):
        def mm(cast=cast):
            if cast:
                wb = wt_ref[0].astype(BF16)
                wb_ref[...] = wb
            else:
                wb = wb_ref[...]
            return lax.dot_general(a_ref[...], wb, (((1,), (1,)), ((), ())),
                                   preferred_element_type=F32)

        step = first if cast else jnp.logical_not(first)

        @pl.when(jnp.logical_and(step, is_q))
        def _(mm=mm):
            o_ref[...] = (mm() * q_scale).astype(o_ref.dtype)

        @pl.when(jnp.logical_and(step, is_g))
        def _(mm=mm):
            o_ref[...] = (_silu(mm()) * gain_ref[...]).astype(o_ref.dtype)

        @pl.when(jnp.logical_and(step, is_plain))
        def _(mm=mm):
            o_ref[...] = mm().astype(o_ref.dtype)


def _in_proj_call(a, w_in_t, gain, layer, tm, tn, name):
    m, k = a.shape
    d = gain.shape[-1]
    d_key = d // 2
    n_head = 2 * d_key + 2 * d
    n = w_in_t.shape[1] - 2 * GATE_RANK
    n_a = n_head // tn
    g_lo = (2 * d_key + d) // tn
    g_hi = g_lo + d // tn
    kern = functools.partial(_in_proj_kernel, n_q=d_key // tn, g_lo=g_lo, g_hi=g_hi,
                             q_scale=(d_key // N_HEADS) ** -0.5)

    def row_start(j):
        sub = 8
        return (j * (tn // sub) + jnp.where(j >= n_a, 2 * GATE_RANK // sub, 0)) * sub

    return pl.pallas_call(
        kern,
        out_shape=jax.ShapeDtypeStruct((m, n), BF16),
        grid=(n // tn, m // tm),
        in_specs=[pl.BlockSpec((tm, k), lambda j, i: (i, 0)),
                  pl.BlockSpec((pl.Element(1), pl.Element(tn), pl.Element(k)),
                               lambda j, i: (layer, row_start(j), 0)),
                  pl.BlockSpec((None, 1, tn),
                               lambda j, i: (layer, 0, jnp.clip(j - g_lo, 0, g_hi - g_lo - 1)))],
        out_specs=pl.BlockSpec((tm, tn), lambda j, i: (i, j)),
        scratch_shapes=[pltpu.VMEM((tn, k), BF16)],
        compiler_params=_cparams(2),
        name=name,
    )(a, w_in_t, gain)


def _in_proj_decay_kernel(a_ref, wt_ref, o_ref, *, n_valid):
    y = lax.dot_general(a_ref[...], wt_ref[...].astype(BF16), (((1,), (1,)), ((), ())),
                        preferred_element_type=F32)
    lane = lax.broadcasted_iota(jnp.int32, y.shape, 1)
    o_ref[...] = jnp.where(lane < n_valid, y, 0.0)


def _in_proj_decay_call(a, w_in_t, layer, tm, d, name):
    m, k = a.shape
    row0 = 2 * (d // 2) + 2 * d
    assert row0 % LANES == 0
    return pl.pallas_call(
        functools.partial(_in_proj_decay_kernel, n_valid=2 * GATE_RANK),
        out_shape=jax.ShapeDtypeStruct((m, LANES), F32),
        grid=(m // tm,),
        in_specs=[pl.BlockSpec((tm, k), lambda i: (i, 0)),
                  pl.BlockSpec((None, LANES, k), lambda i: (layer, row0 // LANES, 0))],
        out_specs=pl.BlockSpec((tm, LANES), lambda i: (i, 0)),
        compiler_params=_cparams(1),
        name=name,
    )(a, w_in_t)


def _gla_kernel(qf_ref, kf_ref, vf_ref, af_ref, qb_ref, kb_ref, vb_ref, ab_ref, g_ref,
                qc_ref, kc_ref, vc_ref, ac_ref, gc_ref,
                wup_ref, bup_ref,
                out_ref, outc_ref,
                state_ref, acc_ref, accc_ref, *, nb):
    s = pl.program_id(2)
    fin_rows = out_ref.shape[0]

    row = lax.broadcasted_iota(jnp.int32, (CHUNK, CHUNK), 0)
    col = lax.broadcasted_iota(jnp.int32, (CHUNK, CHUNK), 1)
    keep = (row >= col, row <= col)
    row2 = lax.broadcasted_iota(jnp.int32, (CHUNK, 2 * CHUNK), 0)
    col2 = jnp.bitwise_and(lax.broadcasted_iota(jnp.int32, (CHUNK, 2 * CHUNK), 1), CHUNK - 1)
    keep2 = (row2 >= col2, row2 <= col2)
    ref_row = (CHUNK // 2 - 1, CHUNK // 2)
    last_row = (CHUNK - 1, 0)

    def log_decay(d, ar):
        z = jnp.dot(ar[...].astype(BF16), wup_ref[d], preferred_element_type=F32) + bup_ref[d]
        log_a = ((jnp.minimum(z, 0.0) - jnp.log(1.0 + jnp.exp2(jnp.abs(z) * (-LOG2E))))
                 * (LOG2E / GATE_NORM))
        hi = log_a.astype(BF16)
        return hi, (log_a - hi.astype(F32)).astype(BF16)

    def group_intra(d, tri, hi, lo, qr, kr, vr, g0):
        n = GLA_GROUP
        scan = list(range(n - 1, -1, -1) if d == 1 else range(n))
        cum, last, q, k, diag = {}, {}, {}, {}, {}
        for a in range(n):
            sl = slice((g0 + a) * CHUNK, (g0 + a + 1) * CHUNK)
            cum[a] = jnp.dot(tri, jnp.concatenate([hi[sl], lo[sl]], axis=0), preferred_element_type=F32)
            ref = cum[a][ref_row[d]:ref_row[d] + 1, :]
            last[a] = cum[a][last_row[d]:last_row[d] + 1, :]
            q[a] = qr[sl, :].astype(F32)
            k[a] = kr[sl, :].astype(F32)
            q_in = (q[a] * jnp.exp2(cum[a] - ref)).astype(BF16)
            k_in = (k[a] * jnp.exp2(ref - cum[a])).astype(BF16)
            s = lax.dot_general(q_in, k_in, (((1,), (1,)), ((), ())), preferred_element_type=F32)
            diag[a] = jnp.where(keep[d], s, 0.0)
        before = [None] * n
        after = [None] * n
        run = None
        for i in range(n):
            before[i] = run
            run = last[scan[i]] if run is None else run + last[scan[i]]
        total = run
        run = None
        for i in range(n - 1, -1, -1):
            after[i] = run
            run = last[scan[i]] if run is None else run + last[scan[i]]

        def scaled(x, log_factor):
            return (x if log_factor is None else x * jnp.exp2(log_factor)).astype(BF16)

        q_dec = {a: q[a] * jnp.exp2(cum[a]) for a in range(n)}
        k_dec = {a: k[a] * jnp.exp2(last[a] - cum[a]) for a in range(n)}
        k_plain = {a: k_dec[a].astype(BF16) for a in range(n)}
        zero_blk = jnp.zeros((CHUNK, CHUNK), F32)
        rows = [None] * n
        for i in range(n):
            ai = scan[i]
            blocks = {ai: diag[ai]}
            between = None
            for j in range(i - 1, -1, -1):
                aj = scan[j]
                blocks[aj] = lax.dot_general(scaled(q_dec[ai], between), k_plain[aj],
                                             (((1,), (1,)), ((), ())), preferred_element_type=F32)
                between = last[aj] if between is None else between + last[aj]
            rows[ai] = jnp.concatenate([blocks.get(a, zero_blk) for a in range(n)], axis=1).astype(BF16)
        p_mat = jnp.concatenate(rows, axis=0)
        q_state = jnp.concatenate([scaled(q_dec[a], before[scan.index(a)]) for a in range(n)], axis=0)
        k_state = jnp.concatenate([scaled(k_dec[a], after[scan.index(a)]) for a in range(n)], axis=0)
        v = vr[g0 * CHUNK:(g0 + n) * CHUNK, :]
        o_intra = jnp.dot(p_mat, v, preferred_element_type=F32)
        upd = lax.dot_general(v, k_state, (((0,), (0,)), ((), ())), preferred_element_type=F32)
        return o_intra, q_state, upd, jnp.exp2(total)

    def scan_pair(fwd, bwd, acc, off_f, off_b):
        streams = []
        for d, (qr, kr, vr, ar), off in ((0, fwd, off_f), (1, bwd, off_b)):
            n_groups = qr.shape[0] // (GLA_GROUP * CHUNK)
            order = list(range(n_groups - 1, -1, -1) if d == 1 else range(n_groups))
            hi, lo = log_decay(d, ar)
            tri = jnp.where(keep2[d], 1.0, 0.0).astype(BF16)
            streams.append((d, tri, qr, kr, vr, hi, lo, order, off))
        n_steps = len(streams[0][7])
        pre = {}
        st = [state_ref[0], state_ref[1]]
        rows = GLA_GROUP * CHUNK
        for i in range(n_steps):
            for d, tri, qr, kr, vr, hi, lo, order, _ in streams:
                pre[d, i] = group_intra(d, tri, hi, lo, qr, kr, vr, order[i] * GLA_GROUP)
            for d, _, _, _, _, _, _, order, off in streams:
                o_intra, q_state, upd, decay = pre[d, i]
                o = o_intra + lax.dot_general(q_state, st[d].astype(BF16), (((1,), (1,)), ((), ())),
                                              preferred_element_type=F32)
                st[d] = st[d] * decay + upd
                ra = pl.multiple_of(off + order[i] * rows, rows)
                acc[pl.ds(ra, rows), :] += o
        state_ref[0] = st[0]
        state_ref[1] = st[1]

    def finalize(acc, base, gr, outr):
        rows = min(GLA_NORM_ROWS, outr.shape[0])

        def body(c, carry):
            r0 = pl.multiple_of(c * rows, rows)
            o = acc[pl.ds(pl.multiple_of(base + r0, rows), rows), :]
            ms = jnp.mean(o * o, axis=-1, keepdims=True)
            y = o * lax.rsqrt(ms + RMS_EPS) * gr[pl.ds(r0, rows), :].astype(F32)
            outr[pl.ds(r0, rows), :] = y.astype(outr.dtype)
            return carry

        lax.fori_loop(0, outr.shape[0] // rows, body, 0)

    ctx = (qc_ref, kc_ref, vc_ref, ac_ref)

    @pl.when(s == 0)
    def _():
        state_ref[...] = jnp.zeros_like(state_ref)
        acc_ref[...] = jnp.zeros_like(acc_ref)
        accc_ref[...] = jnp.zeros_like(accc_ref)
        scan_pair(ctx, ctx, accc_ref, 0, 0)

    @pl.when(s < nb)
    def _():
        scan_pair((qf_ref, kf_ref, vf_ref, af_ref), (qb_ref, kb_ref, vb_ref, ab_ref), acc_ref,
                  s * GLA_BLOCK, (nb - 1 - s) * GLA_BLOCK)

    @pl.when(s == nb)
    def _():
        finalize(accc_ref, 0, gc_ref, outc_ref)

    @pl.when(s >= nb)
    def _():
        finalize(acc_ref, (s - nb) * fin_rows, g_ref, out_ref)


def _gla_call(proj_l, alr_l, proj_c, alr_c, wup, bup, layer, batch, seq, ctx_len, d_model):
    d_key = d_model // 2
    head_k = d_key // N_HEADS
    head_v = d_model // N_HEADS
    nb = seq // GLA_BLOCK
    fin_rows = min(GLA_FINAL_ROWS, seq)
    nf = seq // fin_rows
    kq = d_key // head_k
    kv = (2 * d_key) // head_v
    kg = kv + d_model // head_v

    def blk(s, rev):
        return jnp.maximum(nb - 1 - s, 0) if rev else jnp.minimum(s, nb - 1)

    def lat(width, col0, rev):
        return pl.BlockSpec((GLA_BLOCK, width), lambda b, h, s: (b * nb + blk(s, rev), col0 + h))

    def lat_a(rev):
        return pl.BlockSpec((GLA_BLOCK, LANES), lambda b, h, s: (b * nb + blk(s, rev), 0))

    def fin(col0):
        return pl.BlockSpec((fin_rows, head_v),
                            lambda b, h, s: (b * nf + jnp.clip(s - nb, 0, nf - 1), col0 + h))

    def cx(width, col0):
        return pl.BlockSpec((ctx_len, width), lambda b, h, s: (b, col0 + h))

    in_specs = []
    for rev in (False, True):
        in_specs += [lat(head_k, 0, rev), lat(head_k, kq, rev), lat(head_v, kv, rev), lat_a(rev)]
    in_specs += [fin(kg),
                 cx(head_k, 0), cx(head_k, kq), cx(head_v, kv),
                 pl.BlockSpec((ctx_len, LANES), lambda b, h, s: (b, 0)),
                 cx(head_v, kg),
                 pl.BlockSpec((None, 2, LANES, head_k), lambda b, h, s: (layer, 0, 0, h)),
                 pl.BlockSpec((None, 2, 1, head_k), lambda b, h, s: (layer, 0, 0, h))]
    out_specs = [fin(0), cx(head_v, 0)]
    return pl.pallas_call(
        functools.partial(_gla_kernel, nb=nb),
        out_shape=(jax.ShapeDtypeStruct((batch * seq, d_model), BF16),
                   jax.ShapeDtypeStruct((batch * ctx_len, d_model), BF16)),
        grid=(batch, N_HEADS, nb + nf),
        in_specs=in_specs,
        out_specs=out_specs,
        scratch_shapes=[pltpu.VMEM((2, head_v, head_k), F32),
                        pltpu.VMEM((seq, head_v), F32),
                        pltpu.VMEM((ctx_len, head_v), F32)],
        compiler_params=_cparams(3),
        name="gla",
    )(proj_l, proj_l, proj_l, alr_l, proj_l, proj_l, proj_l, alr_l, proj_l,
      proj_c, proj_c, proj_c, alr_c, proj_c, wup, bup)


def _win(w):
    lo = w // 2
    return lo, w - lo - 1


def _band_matrix(n_tokens, w):
    lo, hi = _win(w)
    t = np.arange(n_tokens)
    off = t[None, :] - t[:, None]
    return ((off >= -lo) & (off <= hi)).astype(np.float32)


def _box_offsets(w):
    lo, hi = _win(w)
    rpt = POOL_TILE // GRID_W
    return list(range(-((lo + rpt - 1) // rpt), (hi + rpt - 1) // rpt + 1))


def _box_matrix(w, tile_offset):
    lo, hi = _win(w)
    rpt = POOL_TILE // GRID_W
    t = np.arange(POOL_TILE)
    a, c = t // GRID_W, t % GRID_W
    drow = rpt * tile_offset + a[None, :] - a[:, None]
    dcol = c[None, :] - c[:, None]
    return ((drow >= -lo) & (drow <= hi) & (dcol >= -lo) & (dcol <= hi)).astype(np.float32)


def _count(idx, n, w):
    lo, hi = _win(w)
    return jnp.minimum(idx + hi + 1, n) - jnp.maximum(idx - lo, 0)


def _pool_lat_kernel(p_ref, box_ref, wg_ref, sc_ref, o_ref, ps_ref, *, n_rows):
    g = pl.program_id(1)
    seq = n_rows * GRID_W
    halo = POOL_HALO_TILES * POOL_TILE
    n_tiles = seq // POOL_TILE
    zeros = jnp.zeros((halo, ps_ref.shape[1]), ps_ref.dtype)
    ps_ref[pl.ds(0, halo), :] = zeros
    ps_ref[pl.ds(halo + seq, halo), :] = zeros
    ps_ref[pl.ds(halo, seq), :] = p_ref[...]

    for gi, w in enumerate(POOL_WINDOWS):
        offsets = _box_offsets(w)

        @pl.when(g == gi)
        def _(w=w, offsets=offsets):
            def tile(t, carry):
                r0 = pl.multiple_of(t * POOL_TILE, POOL_TILE)
                tot = None
                for j, off in enumerate(offsets):
                    src = ps_ref[pl.ds(halo + r0 + off * POOL_TILE, POOL_TILE), :]
                    part = jnp.dot(box_ref[j], src, preferred_element_type=F32)
                    tot = part if tot is None else tot + part
                tok = r0 + lax.broadcasted_iota(jnp.int32, (POOL_TILE, 1), 0)
                img_row = lax.shift_right_logical(tok, GRID_W.bit_length() - 1)
                img_col = jnp.bitwise_and(tok, GRID_W - 1)
                cnt = _count(img_row, n_rows, w) * _count(img_col, GRID_W, w)
                pin = p_ref[pl.ds(r0, POOL_TILE), :].astype(F32)
                diff = (tot * (1.0 / cnt.astype(F32)) - pin).astype(BF16)
                y = jnp.dot(diff, wg_ref[...], preferred_element_type=F32) * sc_ref[...]
                o_ref[pl.ds(r0, POOL_TILE), :] = y.astype(o_ref.dtype)
                return carry

            lax.fori_loop(0, n_tiles, tile, 0, unroll=4)


def _pool_lat_call(proj_l, box, wpg, pscale, layer, batch, seq, d_model):
    d_pool = d_model // 2
    grp = d_pool // len(POOL_WINDOWS)
    col0 = (d_model // 2 * 2 + 2 * d_model) // grp
    halo = POOL_HALO_TILES * POOL_TILE
    return pl.pallas_call(
        functools.partial(_pool_lat_kernel, n_rows=seq // GRID_W),
        out_shape=jax.ShapeDtypeStruct((batch * seq, d_pool), BF16),
        grid=(batch, len(POOL_WINDOWS)),
        in_specs=[pl.BlockSpec((seq, grp), lambda b, g: (b, col0 + g)),
                  pl.BlockSpec((None, box.shape[1], POOL_TILE, POOL_TILE), lambda b, g: (g, 0, 0, 0)),
                  pl.BlockSpec((None, None, grp, grp), lambda b, g: (layer, g, 0, 0)),
                  pl.BlockSpec((None, 1, grp), lambda b, g: (layer, 0, g))],
        out_specs=pl.BlockSpec((seq, grp), lambda b, g: (b, g)),
        scratch_shapes=[pltpu.VMEM((seq + 2 * halo, grp), BF16)],
        compiler_params=_cparams(2),
        name="pool_lat",
    )(proj_l, box, wpg, pscale)


def _pool_ctx_kernel(p_ref, band_ref, icnt_ref, wg_ref, sc_ref, o_ref):
    tot = jnp.dot(band_ref[...], p_ref[...], preferred_element_type=F32)
    diff = (tot * icnt_ref[...] - p_ref[...].astype(F32)).astype(BF16)
    y = jnp.dot(diff, wg_ref[...], preferred_element_type=F32) * sc_ref[...]
    o_ref[...] = y.astype(o_ref.dtype)


def _pool_ctx_call(proj_c, band, icnt, wpg, pscale, layer, batch, ctx_len, d_model):
    d_pool = d_model // 2
    grp = d_pool // len(POOL_WINDOWS)
    col0 = (d_model // 2 * 2 + 2 * d_model) // grp
    return pl.pallas_call(
        _pool_ctx_kernel,
        out_shape=jax.ShapeDtypeStruct((batch * ctx_len, d_pool), BF16),
        grid=(batch, len(POOL_WINDOWS)),
        in_specs=[pl.BlockSpec((ctx_len, grp), lambda b, g: (b, col0 + g)),
                  pl.BlockSpec((None, ctx_len, ctx_len), lambda b, g: (g, 0, 0)),
                  pl.BlockSpec((None, ctx_len, 1), lambda b, g: (g, 0, 0)),
                  pl.BlockSpec((None, None, grp, grp), lambda b, g: (layer, g, 0, 0)),
                  pl.BlockSpec((None, 1, grp), lambda b, g: (layer, 0, g))],
        out_specs=pl.BlockSpec((ctx_len, grp), lambda b, g: (b, g)),
        compiler_params=_cparams(2),
        name="pool_ctx",
    )(proj_c, band, icnt, wpg, pscale)


def _merge_kernel(ga_ref, pa_ref, bgg_ref, bgp_ref, wg_ref, wp_ref, o_ref, wgb_ref, wpb_ref):
    first = pl.program_id(1) == 0

    def merge(wg, wp):
        yg = jnp.dot(ga_ref[...], wg, preferred_element_type=F32)
        yp = jnp.dot(pa_ref[...], wp, preferred_element_type=F32)
        y = (jax.nn.sigmoid(bgg_ref[...].astype(F32)) * yg
             + jax.nn.sigmoid(bgp_ref[...].astype(F32)) * yp)
        o_ref[...] = y.astype(o_ref.dtype)

    @pl.when(first)
    def _():
        wg = wg_ref[...].astype(BF16)
        wp = wp_ref[...].astype(BF16)
        wgb_ref[...] = wg
        wpb_ref[...] = wp
        merge(wg, wp)

    @pl.when(jnp.logical_not(first))
    def _():
        merge(wgb_ref[...], wpb_ref[...])


def _merge_call(gla_act, pool_act, proj, w_gla_out, w_pool_out, layer, tm, tn, n_rows=None):
    m = gla_act.shape[0] if n_rows is None else n_rows
    d = w_gla_out.shape[-1]
    d_pool = pool_act.shape[1]
    bg0 = (proj.shape[1] - 2 * d) // tn
    return pl.pallas_call(
        _merge_kernel,
        out_shape=jax.ShapeDtypeStruct((m, d), BF16),
        grid=(d // tn, m // tm),
        in_specs=[pl.BlockSpec((tm, d), lambda j, i: (i, 0)),
                  pl.BlockSpec((tm, d_pool), lambda j, i: (i, 0)),
                  pl.BlockSpec((tm, tn), lambda j, i: (i, bg0 + j)),
                  pl.BlockSpec((tm, tn), lambda j, i: (i, bg0 + d // tn + j)),
                  pl.BlockSpec((None, d, tn), lambda j, i: (layer, 0, j)),
                  pl.BlockSpec((None, d_pool, tn), lambda j, i: (layer, 0, j))],
        out_specs=pl.BlockSpec((tm, tn), lambda j, i: (i, j)),
        scratch_shapes=[pltpu.VMEM((d, tn), BF16), pltpu.VMEM((d_pool, tn), BF16)],
        compiler_params=_cparams(2),
        name="merge",
    )(gla_act, pool_act, proj, proj, w_gla_out, w_pool_out)


def _mm_ln_kernel(*refs, row_fn, nm, nk, alpha, emit_h):
    if emit_h:
        (a_ref, w_ref, x_ref, gt_ref, lng_ref, lnb_ref, sc_ref, sh_ref, xo_ref, ho_ref, *acc) = refs
    else:
        (a_ref, w_ref, x_ref, gt_ref, lng_ref, lnb_ref, xo_ref, *acc) = refs
    i = pl.program_id(0)
    kk = pl.program_id(1)
    tq = x_ref.shape[0] // nk

    def matmul(acc_ref):
        part = jnp.dot(a_ref[...], w_ref[...], preferred_element_type=F32)
        if nk == 1:
            acc_ref[...] = part
        else:
            acc_ref[...] += part

    def epilogue(acc_ref):
        base = kk * tq
        r = row_fn(i - 1)
        gt = gt_ref[pl.ds(r, 1), :]
        lng = lng_ref[...]
        lnb = lnb_ref[...]
        if emit_h:
            sc1 = 1.0 + sc_ref[pl.ds(r, 1), :]
            sh = sh_ref[pl.ds(r, 1), :]
        for c in range(tq // LN_ROWS):
            rows = pl.ds(pl.multiple_of(base + c * LN_ROWS, LN_ROWS), LN_ROWS)
            y = alpha * x_ref[rows, :] + gt * acc_ref[rows, :]
            mu = jnp.mean(y, axis=-1, keepdims=True)
            yc = y - mu
            var = jnp.mean(yc * yc, axis=-1, keepdims=True)
            xn = yc * lax.rsqrt(var + LN_EPS) * lng + lnb
            xo_ref[rows, :] = xn
            if emit_h:
                ho_ref[rows, :] = (xn * sc1 + sh).astype(BF16)
            if nk > 1:
                acc_ref[rows, :] = jnp.zeros((LN_ROWS, acc_ref.shape[1]), F32)

    @pl.when(i == 0)
    def _():
        if nk > 1:
            @pl.when(kk == 0)
            def _():
                acc[0][...] = jnp.zeros_like(acc[0])
                acc[1][...] = jnp.zeros_like(acc[1])
        matmul(acc[0])

    for parity in (0, 1):
        @pl.when(jnp.logical_and(jnp.logical_and(i >= 1, i < nm), i % 2 == parity))
        def _(parity=parity):
            epilogue(acc[1 - parity])
            matmul(acc[parity])

    @pl.when(i == nm)
    def _():
        epilogue(acc[(nm - 1) % 2])


def _mm_ln_call(a, w, x, mods, ln_gain, ln_bias, layer, gate_chunk, next_mod, row_fn, tm, tk, alpha,
                n_rows=None):
    m = x.shape[0] if n_rows is None else n_rows
    k = a.shape[1]
    d = x.shape[1]
    nk = k // tk
    nm = m // tm
    emit_h = next_mod is not None

    def k_idx(i, kk):
        return jnp.where(i < nm, kk, nk - 1)

    prev = lambda i, kk: (jnp.maximum(i - 1, 0), 0)
    in_specs = [pl.BlockSpec((tm, tk), lambda i, kk: (jnp.minimum(i, nm - 1), k_idx(i, kk))),
                pl.BlockSpec((None, tk, d), lambda i, kk: (layer, k_idx(i, kk), 0)),
                pl.BlockSpec((tm, d), prev),
                _mod_spec(layer, gate_chunk, d),
                pl.BlockSpec((None, 1, d), lambda i, kk: (layer, 0, 0)),
                pl.BlockSpec((None, 1, d), lambda i, kk: (layer, 0, 0))]
    args = [a, w, x, mods, ln_gain, ln_bias]
    out_shape = [jax.ShapeDtypeStruct((m, d), F32)]
    out_specs = [pl.BlockSpec((tm, d), prev)]
    if emit_h:
        nl, sc_chunk, sh_chunk = next_mod
        in_specs += [_mod_spec(nl, sc_chunk, d), _mod_spec(nl, sh_chunk, d)]
        args += [mods, mods]
        out_shape.append(jax.ShapeDtypeStruct((m, d), BF16))
        out_specs.append(pl.BlockSpec((tm, d), prev))
    outs = pl.pallas_call(
        functools.partial(_mm_ln_kernel, row_fn=row_fn, nm=nm, nk=nk, alpha=alpha, emit_h=emit_h),
        out_shape=out_shape,
        grid=(nm + 1, nk),
        in_specs=in_specs,
        out_specs=out_specs,
        scratch_shapes=[pltpu.VMEM((tm, d), F32), pltpu.VMEM((tm, d), F32)],
        compiler_params=_cparams(2),
        name="mm_ln",
    )(*args)
    return (outs[0], outs[1]) if emit_h else (outs[0], None)


def _ffn1_kernel(h_ref, wg_ref, wu_ref, o_ref):
    h = h_ref[...]
    gate = jnp.dot(h, wg_ref[...].astype(BF16), preferred_element_type=F32)
    up = jnp.dot(h, wu_ref[...].astype(BF16), preferred_element_type=F32)
    o_ref[...] = (_silu(gate) * up).astype(o_ref.dtype)


def _ffn1_call(h, w_ffn_in, layer, tm, tn, n_rows=None):
    m = h.shape[0] if n_rows is None else n_rows
    d = h.shape[1]
    d_ff = w_ffn_in.shape[-1] // 2
    nj = d_ff // tn
    return pl.pallas_call(
        _ffn1_kernel,
        out_shape=jax.ShapeDtypeStruct((m, d_ff), BF16),
        grid=(m // tm, nj),
        in_specs=[pl.BlockSpec((tm, d), lambda i, j: (i, 0)),
                  pl.BlockSpec((None, d, tn), lambda i, j: (layer, 0, j)),
                  pl.BlockSpec((None, d, tn), lambda i, j: (layer, 0, nj + j))],
        out_specs=pl.BlockSpec((tm, tn), lambda i, j: (i, j)),
        compiler_params=_cparams(2),
        name="ffn1",
    )(h, w_ffn_in, w_ffn_in)


def kernel(x, c, ctx, c_ctx, w_ada, b_ada, w_in, w_decay_up, b_decay_up, gla_norm_gain, w_pool_group, pool_scale, w_gla_out, w_pool_out, w_out, ln_mix_gain, ln_mix_bias, w_ffn_in, w_ffn_out, ln_ffn_gain, ln_ffn_bias):
    batch, seq, d = x.shape
    ctx_len = ctx.shape[1]
    depth = w_ada.shape[0]
    d_key = d // 2
    d_pool = d // 2
    d_ff = w_ffn_out.shape[1]
    alpha = (2.0 * depth) ** 0.25
    m_lat = batch * seq
    m_ctx = batch * ctx_len
    assert seq % (2 * GLA_BLOCK) == 0 and seq % LAT_TM == 0
    assert ctx_len % (GLA_GROUP * CHUNK) == 0 and GLA_BLOCK % (GLA_GROUP * CHUNK) == 0
    assert batch + 1 <= 8

    x_l = x.reshape(m_lat, d)
    x_c = ctx.reshape(m_ctx, d)
    cond = jnp.zeros((8, d), F32).at[:batch].set(c).at[batch].set(c_ctx)
    w_in_t = jnp.swapaxes(w_in, 1, 2)
    wup = jnp.zeros((depth, 2, LANES, d_key), F32)
    wup = wup.at[:, 0, :GATE_RANK].set(w_decay_up[:, 0]).at[:, 1, GATE_RANK:2 * GATE_RANK].set(w_decay_up[:, 1])
    wup = wup.astype(BF16)
    bup = b_decay_up.reshape(depth, 2, 1, d_key)
    gain = gla_norm_gain.reshape(depth, 1, d)
    wpg = w_pool_group.astype(BF16)
    pscale = pool_scale.reshape(depth, 1, d_pool)
    w_out_b = w_out.astype(BF16)
    w_ffn_out_b = w_ffn_out.astype(BF16)
    lnm_g = ln_mix_gain.reshape(depth, 1, d)
    lnm_b = ln_mix_bias.reshape(depth, 1, d)
    lnf_g = ln_ffn_gain.reshape(depth, 1, d)
    lnf_b = ln_ffn_bias.reshape(depth, 1, d)
    n_off = max(len(_box_offsets(w)) for w in POOL_WINDOWS)
    assert max(abs(o) for w in POOL_WINDOWS for o in _box_offsets(w)) <= POOL_HALO_TILES
    box_np = np.zeros((len(POOL_WINDOWS), n_off, POOL_TILE, POOL_TILE), np.float32)
    for gi, w in enumerate(POOL_WINDOWS):
        for j, off in enumerate(_box_offsets(w)):
            box_np[gi, j] = _box_matrix(w, off)
    box_lat = jnp.asarray(box_np, BF16)
    band_ctx = jnp.asarray(np.stack([_band_matrix(ctx_len, w) for w in POOL_WINDOWS]), BF16)
    t = np.arange(ctx_len)
    icnt_ctx = jnp.asarray(np.stack([
        1.0 / (np.minimum(t + _win(w)[1] + 1, ctx_len) - np.maximum(t - _win(w)[0], 0))
        for w in POOL_WINDOWS]).astype(np.float32)[:, :, None])

    mods = _mods_call(cond, w_ada, b_ada)

    lat_row_1024 = lambda i: i // (seq // LAT_TM)
    lat_row_512 = lambda i: i // (seq // LN_TM)
    ctx_row = lambda i: batch

    h_l = _modulate_call(x_l, mods, 0, lat_row_1024, LAT_TM)
    h_c = _modulate_call(x_c, mods, 0, ctx_row, m_ctx)

    ffn_tk = d_ff // 4
    for layer in range(depth):
        ctx_out = layer < depth - 1
        proj_l = _in_proj_call(h_l, w_in_t, gain, layer, LAT_TM, 1024, "in_proj")
        alr_l = _in_proj_decay_call(h_l, w_in_t, layer, LAT_TM, d, "in_proj_decay")
        proj_c = _in_proj_call(h_c, w_in_t, gain, layer, m_ctx, 1024, "in_proj_ctx")
        alr_c = _in_proj_decay_call(h_c, w_in_t, layer, m_ctx, d, "in_proj_decay_ctx")
        gla_l, gla_c = _gla_call(proj_l, alr_l, proj_c, alr_c, wup, bup, layer,
                                 batch, seq, ctx_len, d)
        pool_l = _pool_lat_call(proj_l, box_lat, wpg, pscale, layer, batch, seq, d)
        merged_l = _merge_call(gla_l, pool_l, proj_l, w_gla_out, w_pool_out, layer, LN_TM, 1024)
        x_l, hf_l = _mm_ln_call(merged_l, w_out_b, x_l, mods, lnm_g, lnm_b, layer, 2,
                                (layer, 4, 3), lat_row_512, LN_TM, d, alpha)
        act_l = _ffn1_call(hf_l, w_ffn_in, layer, LAT_TM, 512)
        nxt = (layer + 1, 1, 0) if ctx_out else None
        x_l, h_l = _mm_ln_call(act_l, w_ffn_out_b, x_l, mods, lnf_g, lnf_b, layer, 5,
                               nxt, lat_row_512, LN_TM, ffn_tk, alpha)
        if ctx_out:
            pool_c = _pool_ctx_call(proj_c, band_ctx, icnt_ctx, wpg, pscale, layer, batch, ctx_len, d)
            merged_c = _merge_call(gla_c, pool_c, proj_c, w_gla_out, w_pool_out, layer, m_ctx, 1024)
            x_c, hf_c = _mm_ln_call(merged_c, w_out_b, x_c, mods, lnm_g, lnm_b, layer, 2,
                                    (layer, 4, 3), ctx_row, m_ctx, d, alpha)
            act_c = _ffn1_call(hf_c, w_ffn_in, layer, m_ctx, 512)
            x_c, h_c = _mm_ln_call(act_c, w_ffn_out_b, x_c, mods, lnf_g, lnf_b, layer, 5,
                                   (layer + 1, 1, 0), ctx_row, m_ctx, ffn_tk, alpha)
    return x_l.reshape(batch, seq, d)
```
